```python
import jax, jax.numpy as jnp
from jax import lax
import numpy as np

D_MODEL = 1024
BATCH = 8
SEQ = 2048
DEPTH = 4
DEC_BATCH = 1
DEC_SEQ = 16384
PAST_LEN = 128

N_MEM = 256
D_FF = 2816
FOURIER_GROUPS = 4
FOURIER_GROUP_DIM = 128
FOURIER_WIDTH = FOURIER_GROUPS * FOURIER_GROUP_DIM
RET_HEADS = 4
RET_DK = 128
RET_DV = 256
RET_QK_WIDTH = RET_HEADS * RET_DK
RET_V_WIDTH = RET_HEADS * RET_DV
RET_CHUNK = 128
N_BRANCHES = 2
IN_WIDTH = FOURIER_WIDTH + 2 * RET_QK_WIDTH + 2 * RET_V_WIDTH + N_BRANCHES * D_MODEL
IN_SPLITS = (FOURIER_WIDTH,
             FOURIER_WIDTH + RET_QK_WIDTH,
             FOURIER_WIDTH + 2 * RET_QK_WIDTH,
             FOURIER_WIDTH + 2 * RET_QK_WIDTH + RET_V_WIDTH,
             FOURIER_WIDTH + 2 * RET_QK_WIDTH + 2 * RET_V_WIDTH)
XA_HEADS = 4
XA_HEAD_DIM = D_MODEL // XA_HEADS
ROPE_BASE = 10000.0
EPS = 1e-6

kernel_name = "hybrid_fnet_retention_encoder"


def rmsnorm(x, g):
    xf = x.astype(jnp.float32)
    y = xf * lax.rsqrt(jnp.mean(xf * xf, axis=-1, keepdims=True) + EPS)
    return (y * g.astype(jnp.float32)).astype(x.dtype)


def swiglu(h, w_in, w_out):
    gate, up = jnp.split(h @ w_in, 2, axis=-1)
    return (jax.nn.silu(gate) * up) @ w_out


def rotary(x, pos):
    d = x.shape[-1]
    half = d // 2
    inv = 1.0 / (ROPE_BASE ** (jnp.arange(half, dtype=jnp.float32) * 2.0 / d))
    ang = pos[:, None] * inv[None, :]
    cos = jnp.cos(ang)[None, :, None, :]
    sin = jnp.sin(ang)[None, :, None, :]
    x1, x2 = x[..., :half], x[..., half:]
    return jnp.concatenate([x1 * cos - x2 * sin, x2 * cos + x1 * sin], axis=-1)


def fourier_mix(hf):
    B, S, _ = hf.shape
    a = hf.astype(jnp.float32).reshape(B, S, FOURIER_GROUPS, FOURIER_GROUP_DIM)
    y = jnp.fft.fft2(a, axes=(1, 3), norm="ortho").real
    return y.reshape(B, S, FOURIER_WIDTH).astype(hf.dtype)


def retention_dir(q, k, v, log_gamma, strict):
    B, S, H, DK = q.shape
    DV = v.shape[-1]
    C = RET_CHUNK
    N = S // C
    qc = q.reshape(B, N, C, H, DK)
    kc = k.reshape(B, N, C, H, DK)
    vc = v.reshape(B, N, C, H, DV)
    idx = jnp.arange(C, dtype=jnp.float32)
    diff = idx[:, None] - idx[None, :]
    mask = (diff > 0) if strict else (diff >= 0)
    decay = jnp.where(mask[None], jnp.exp(log_gamma[:, None, None] * jnp.where(mask, diff, 0.0)[None]), 0.0)
    scores = jnp.einsum('bnchd,bnmhd->bnhcm', qc, kc) * decay[None, None]
    intra = jnp.einsum('bnhcm,bnmhe->bnche', scores, vc)
    xi = jnp.exp(log_gamma[None, :] * (idx[:, None] + 1.0))
    zeta = jnp.exp(log_gamma[None, :] * (C - 1.0 - idx[:, None]))
    g_chunk = jnp.exp(log_gamma * C)[None, :, None, None]
    upd = jnp.einsum('bnmhd,bnmhe->nbhde', kc * zeta[None, None, :, :, None], vc)

    def step(state, u):
        return g_chunk * state + u, state

    _, r_prev = lax.scan(step, jnp.zeros((B, H, DK, DV), jnp.float32), upd)
    cross = jnp.einsum('bnchd,nbhde->bnche', qc * xi[None, None, :, :, None], r_prev)
    return (intra + cross).reshape(B, S, H, DV)


def retention(hq, hk, hv, hg, decay_fwd, decay_bwd):
    B, S, _ = hq.shape
    pos = jnp.arange(S, dtype=jnp.float32)
    q = rotary(hq.astype(jnp.float32).reshape(B, S, RET_HEADS, RET_DK), pos)
    k = rotary(hk.astype(jnp.float32).reshape(B, S, RET_HEADS, RET_DK), pos) * (RET_DK ** -0.5)
    v = hv.astype(jnp.float32).reshape(B, S, RET_HEADS, RET_DV)
    lg_f = jax.nn.log_sigmoid(decay_fwd.astype(jnp.float32))
    lg_b = jax.nn.log_sigmoid(decay_bwd.astype(jnp.float32))
    y_f = retention_dir(q, k, v, lg_f, False)
    y_b = jnp.flip(retention_dir(jnp.flip(q, 1), jnp.flip(k, 1), jnp.flip(v, 1), lg_b, True), 1)
    y = y_f + y_b
    y = y * lax.rsqrt(jnp.mean(y * y, axis=-1, keepdims=True) + EPS)
    y = y.reshape(B, S, RET_V_WIDTH).astype(hg.dtype)
    return jax.nn.silu(hg) * y


def cross_attention(h, mem_n, wq, wkv, wo):
    B, S, _ = h.shape
    M = mem_n.shape[1]
    q = (h @ wq).reshape(B, S, XA_HEADS, XA_HEAD_DIM)
    k, v = jnp.split(mem_n @ wkv, 2, axis=-1)
    k = k.reshape(B, M, XA_HEADS, XA_HEAD_DIM)
    v = v.reshape(B, M, XA_HEADS, XA_HEAD_DIM)
    s = jnp.einsum('bshd,bmhd->bhsm', q, k).astype(jnp.float32) * (XA_HEAD_DIM ** -0.5)
    p = jax.nn.softmax(s, axis=-1).astype(v.dtype)
    o = jnp.einsum('bhsm,bmhd->bshd', p, v).reshape(B, S, D_MODEL)
    return o @ wo


def trunk(x, mem, ffn1_norm, ffn1_w_in, ffn1_w_out, mix_norm, mix_w_in, fourier_w, ret_decay_fwd,
          ret_decay_bwd, ret_w_out, mix_w_out, xa_norm, mem_norm, xa_wq, xa_wkv, xa_wo,
          ffn2_norm, ffn2_w_in, ffn2_w_out, final_norm):
    for l in range(DEPTH):
        x = x + 0.5 * swiglu(rmsnorm(x, ffn1_norm[l]), ffn1_w_in[l], ffn1_w_out[l])
        h = rmsnorm(x, mix_norm[l])
        hf, hq, hk, hv, hg, gates = jnp.split(h @ mix_w_in[l], IN_SPLITS, axis=-1)
        y_a = fourier_mix(hf) @ fourier_w[l]
        y_b = retention(hq, hk, hv, hg, ret_decay_fwd[l], ret_decay_bwd[l]) @ ret_w_out[l]
        g_a, g_b = jnp.split(jax.nn.sigmoid(gates), 2, axis=-1)
        x = x + (g_a * y_a + g_b * y_b) @ mix_w_out[l]
        x = x + cross_attention(rmsnorm(x, xa_norm[l]), rmsnorm(mem, mem_norm[l]), xa_wq[l], xa_wkv[l], xa_wo[l])
        x = x + 0.5 * swiglu(rmsnorm(x, ffn2_norm[l]), ffn2_w_in[l], ffn2_w_out[l])
    return rmsnorm(x, final_norm)


def setup_inputs(seed: int = 0) -> dict:
    key = jax.random.key(seed)
    ks = jax.random.split(key, 32)
    f32 = jnp.float32

    def w(k, shape, fan_in):
        return jax.random.normal(k, shape, f32) * (fan_in ** -0.5)

    def gain(k, shape):
        return 1.0 + 0.01 * jax.random.normal(k, shape, f32)

    base_logit = jnp.log(2.0 ** (5.0 + jnp.arange(RET_HEADS, dtype=f32)) - 1.0)
    return {
        "x_prompt": jax.random.normal(ks[0], (BATCH, SEQ, D_MODEL), f32),
        "x_sample": jax.random.normal(ks[1], (DEC_BATCH, DEC_SEQ, D_MODEL), f32),
        "mem_prompt": jax.random.normal(ks[2], (BATCH, N_MEM, D_MODEL), f32),
        "mem_sample": jax.random.normal(ks[3], (DEC_BATCH, N_MEM, D_MODEL), f32),
        "ffn1_norm": gain(ks[4], (DEPTH, D_MODEL)),
        "ffn1_w_in": w(ks[5], (DEPTH, D_MODEL, 2 * D_FF), D_MODEL),
        "ffn1_w_out": w(ks[6], (DEPTH, D_FF, D_MODEL), D_FF),
        "mix_norm": gain(ks[7], (DEPTH, D_MODEL)),
        "mix_w_in": w(ks[8], (DEPTH, D_MODEL, IN_WIDTH), D_MODEL),
        "fourier_w": w(ks[9], (DEPTH, FOURIER_WIDTH, D_MODEL), FOURIER_WIDTH),
        "ret_decay_fwd": base_logit[None, :] + 0.05 * jax.random.normal(ks[10], (DEPTH, RET_HEADS), f32),
        "ret_decay_bwd": base_logit[None, :] + 0.05 * jax.random.normal(ks[11], (DEPTH, RET_HEADS), f32),
        "ret_w_out": w(ks[12], (DEPTH, RET_V_WIDTH, D_MODEL), RET_V_WIDTH),
        "mix_w_out": w(ks[13], (DEPTH, D_MODEL, D_MODEL), D_MODEL),
        "xa_norm": gain(ks[14], (DEPTH, D_MODEL)),
        "mem_norm": gain(ks[15], (DEPTH, D_MODEL)),
        "xa_wq": w(ks[16], (DEPTH, D_MODEL, D_MODEL), D_MODEL),
        "xa_wkv": w(ks[17], (DEPTH, D_MODEL, 2 * D_MODEL), D_MODEL),
        "xa_wo": w(ks[18], (DEPTH, D_MODEL, D_MODEL), D_MODEL),
        "ffn2_norm": gain(ks[19], (DEPTH, D_MODEL)),
        "ffn2_w_in": w(ks[20], (DEPTH, D_MODEL, 2 * D_FF), D_MODEL),
        "ffn2_w_out": w(ks[21], (DEPTH, D_FF, D_MODEL), D_FF),
        "final_norm": gain(ks[22], (D_MODEL,)),
    }


def reference(x_prompt, x_sample, mem_prompt, mem_sample, ffn1_norm, ffn1_w_in, ffn1_w_out, mix_norm,
              mix_w_in, fourier_w, ret_decay_fwd, ret_decay_bwd, ret_w_out, mix_w_out, xa_norm, mem_norm,
              xa_wq, xa_wkv, xa_wo, ffn2_norm, ffn2_w_in, ffn2_w_out, final_norm):
    y_prompt = trunk(x_prompt, mem_prompt, ffn1_norm, ffn1_w_in, ffn1_w_out, mix_norm, mix_w_in, fourier_w,
                     ret_decay_fwd, ret_decay_bwd, ret_w_out, mix_w_out, xa_norm, mem_norm, xa_wq, xa_wkv,
                     xa_wo, ffn2_norm, ffn2_w_in, ffn2_w_out, final_norm)
    y_sample = trunk(x_sample, mem_sample, ffn1_norm, ffn1_w_in, ffn1_w_out, mix_norm, mix_w_in, fourier_w,
                     ret_decay_fwd, ret_decay_bwd, ret_w_out, mix_w_out, xa_norm, mem_norm, xa_wq, xa_wkv,
                     xa_wo, ffn2_norm, ffn2_w_in, ffn2_w_out, final_norm)
    return (y_prompt, y_sample)
```

```python
import functools
import math

import jax
import jax.numpy as jnp
from jax import lax
from jax.experimental import pallas as pl
from jax.experimental.pallas import tpu as pltpu

F32 = jnp.float32
BF16 = jnp.bfloat16

D_MODEL = 1024
DEPTH = 4
D_FF = 2816
FOURIER_GROUP_DIM = 128
FOURIER_WIDTH = 512
RET_HEADS = 4
RET_DK = 128
RET_DV = 256
RET_QK_WIDTH = RET_HEADS * RET_DK
RET_V_WIDTH = RET_HEADS * RET_DV
XA_HEADS = 4
XA_HEAD_DIM = D_MODEL // XA_HEADS
ROPE_BASE = 10000.0
EPS = 1e-6

_C_F = (0, FOURIER_WIDTH)
_C_Q = (_C_F[1], _C_F[1] + RET_QK_WIDTH)
_C_K = (_C_Q[1], _C_Q[1] + RET_QK_WIDTH)
_C_V = (_C_K[1], _C_K[1] + RET_V_WIDTH)
_C_G = (_C_V[1], _C_V[1] + RET_V_WIDTH)
_C_GATES = (_C_G[1], _C_G[1] + 2 * D_MODEL)

_V7X_VMEM_BYTES = 64 * 1024 * 1024
_VMEM_LIMIT = _V7X_VMEM_BYTES - 8 * 1024 * 1024

_RET_CHUNK = 256
_DFT_TWO_STAGE_MIN = 16384
_DFT_N1 = 128


def _params(n_axes):
    return pltpu.CompilerParams(
        dimension_semantics=("arbitrary",) * n_axes, vmem_limit_bytes=_VMEM_LIMIT)


def _rmsnorm(x, g):
    ms = jnp.mean(x * x, axis=-1, keepdims=True)
    return x * lax.rsqrt(ms + EPS) * g


def _sigmoid(x):
    return 1.0 / (1.0 + jnp.exp(-x))


def _log_sigmoid(x):
    return -(jnp.maximum(-x, 0.0) + jnp.log1p(jnp.exp(-jnp.abs(x))))


def _dot(a, b):
    return jnp.dot(a, b, preferred_element_type=F32)


def _dot_nt(a, b):
    return lax.dot_general(a, b, (((1,), (1,)), ((), ())), preferred_element_type=F32)


def _dot_tn(a, b):
    return lax.dot_general(a, b, (((0,), (0,)), ((), ())), preferred_element_type=F32)


def _ffn_kernel(x_ref, g_ref, wg_ref, wu_ref, wo_ref, fg_ref, o_ref, xn_ref, acc_ref, *, nj, final_norm):
    j = pl.program_id(1)

    @pl.when(j == 0)
    def _():
        xn_ref[...] = _rmsnorm(x_ref[...], g_ref[...]).astype(BF16)
        acc_ref[...] = jnp.zeros_like(acc_ref)

    xn = xn_ref[...]
    gate = _dot(xn, wg_ref[...])
    up = _dot(xn, wu_ref[...])
    act = (gate * _sigmoid(gate) * up).astype(BF16)
    acc_ref[...] += _dot(act, wo_ref[...])

    @pl.when(j == nj - 1)
    def _():
        y = x_ref[...] + 0.5 * acc_ref[...]
        if final_norm:
            y = _rmsnorm(y, fg_ref[...])
        o_ref[...] = y


def _ffn(x2, gain, wg, wu, wo, final_gain, *, final_norm, tm=512, tf=1408):
    t, d = x2.shape
    f = wg.shape[1]
    nj = f // tf
    return pl.pallas_call(
        functools.partial(_ffn_kernel, nj=nj, final_norm=final_norm),
        grid=(t // tm, nj),
        in_specs=[
            pl.BlockSpec((tm, d), lambda i, j: (i, 0)),
            pl.BlockSpec((1, d), lambda i, j: (0, 0)),
            pl.BlockSpec((d, tf), lambda i, j: (0, j)),
            pl.BlockSpec((d, tf), lambda i, j: (0, j)),
            pl.BlockSpec((tf, d), lambda i, j: (j, 0)),
            pl.BlockSpec((1, d), lambda i, j: (0, 0)),
        ],
        out_specs=pl.BlockSpec((tm, d), lambda i, j: (i, 0)),
        out_shape=jax.ShapeDtypeStruct((t, d), F32),
        scratch_shapes=[pltpu.VMEM((tm, d), BF16), pltpu.VMEM((tm, d), F32)],
        compiler_params=_params(2),
        name="ffn",
    )(x2, gain, wg, wu, wo, final_gain)


def _mix_in_kernel(x_ref, g_ref, w_ref, cos_ref, sin_ref,
                   hf_ref, q_ref, k_ref, v_ref, sg_ref, gt_ref):
    xn = _rmsnorm(x_ref[0], g_ref[...]).astype(BF16)

    def proj(cols):
        return _dot(xn, w_ref[:, cols[0]:cols[1]])

    hf_ref[0] = proj(_C_F).astype(BF16)

    cosf = cos_ref[...]
    sinf = sin_ref[...]

    def rotary(hx, out_ref, scale):
        for h in range(RET_HEADS):
            blk = hx[:, h * RET_DK:(h + 1) * RET_DK]
            rot = blk * cosf + pltpu.roll(blk, RET_DK // 2, 1) * sinf
            if scale != 1.0:
                rot = rot * scale
            out_ref[0, :, h * RET_DK:(h + 1) * RET_DK] = rot.astype(BF16)

    rotary(proj(_C_Q), q_ref, 1.0)
    rotary(proj(_C_K), k_ref, RET_DK ** -0.5)
    v_ref[0] = proj(_C_V).astype(BF16)
    hg = proj(_C_G)
    sg_ref[0] = (hg * _sigmoid(hg)).astype(BF16)
    gt_ref[0] = _sigmoid(proj(_C_GATES)).astype(BF16)


def _mix_in(x3, gain, w, cosf, sinf, *, tm=512):
    b, s, d = x3.shape
    n = w.shape[1]

    def tok(width):
        return pl.BlockSpec((1, tm, width), lambda i, j: (i, j, 0))

    def out(width):
        return jax.ShapeDtypeStruct((b, s, width), BF16)

    return pl.pallas_call(
        _mix_in_kernel,
        grid=(b, s // tm),
        in_specs=[
            tok(d),
            pl.BlockSpec((1, d), lambda i, j: (0, 0)),
            pl.BlockSpec((d, n), lambda i, j: (0, 0)),
            pl.BlockSpec((tm, RET_DK), lambda i, j: (j, 0)),
            pl.BlockSpec((tm, RET_DK), lambda i, j: (j, 0)),
        ],
        out_specs=[tok(FOURIER_WIDTH), tok(RET_QK_WIDTH), tok(RET_QK_WIDTH),
                   tok(RET_V_WIDTH), tok(RET_V_WIDTH), tok(2 * D_MODEL)],
        out_shape=[out(FOURIER_WIDTH), out(RET_QK_WIDTH), out(RET_QK_WIDTH),
                   out(RET_V_WIDTH), out(RET_V_WIDTH), out(2 * D_MODEL)],
        compiler_params=_params(2),
        name="mix_in",
    )(x3, gain, w, cosf, sinf)


def _dft_direct_kernel(a_ref, f_ref, xr_ref, xi_ref):
    s = a_ref.shape[1]
    res = _dot(f_ref[...], a_ref[0])
    xr_ref[0] = res[:s].astype(BF16)
    xi_ref[0] = res[s:].astype(BF16)


def _dft_direct(hf, fmat, *, bw=256):
    b, s, c = hf.shape
    spec = pl.BlockSpec((1, s, bw), lambda i, j: (i, 0, j))
    return pl.pallas_call(
        _dft_direct_kernel,
        grid=(b, c // bw),
        in_specs=[spec, pl.BlockSpec((2 * s, s), lambda i, j: (0, 0))],
        out_specs=[spec, spec],
        out_shape=[jax.ShapeDtypeStruct((b, s, c), BF16)] * 2,
        compiler_params=_params(2),
        name="dft_direct",
    )(hf, fmat)


def _dft_stage1_kernel(a_ref, f_ref, tc_ref, ts_ref, tr_ref, ti_ref, *, bn, n1, c):
    res = _dot(f_ref[...], a_ref[0])
    lw = tc_ref.shape[-1]
    for j in range(bn):
        cw = tc_ref[j]
        sw = ts_ref[j]
        for q in range(c // lw):
            lo = j * c + q * lw
            ur = res[:n1, lo:lo + lw]
            ui = res[n1:, lo:lo + lw]
            tr_ref[0, j, :, q * lw:(q + 1) * lw] = (ur * cw + ui * sw).astype(BF16)
            ti_ref[0, j, :, q * lw:(q + 1) * lw] = (ui * cw - ur * sw).astype(BF16)


def _dft_stage1(hf, f1, twc, tws, *, n1, bn=8):
    b, s, c = hf.shape
    n2 = s // n1
    a = hf.reshape(b, n1, n2 * c)
    lw = twc.shape[-1]
    out_spec = pl.BlockSpec((1, bn, n1, c), lambda i, j: (i, j, 0, 0))
    return pl.pallas_call(
        functools.partial(_dft_stage1_kernel, bn=bn, n1=n1, c=c),
        grid=(b, n2 // bn),
        in_specs=[
            pl.BlockSpec((1, n1, bn * c), lambda i, j: (i, 0, j)),
            pl.BlockSpec((2 * n1, n1), lambda i, j: (0, 0)),
            pl.BlockSpec((bn, n1, lw), lambda i, j: (j, 0, 0)),
            pl.BlockSpec((bn, n1, lw), lambda i, j: (j, 0, 0)),
        ],
        out_specs=[out_spec, out_spec],
        out_shape=[jax.ShapeDtypeStruct((b, n2, n1, c), BF16)] * 2,
        compiler_params=_params(2),
        name="dft_stage1",
    )(a, f1, twc, tws)


def _dft_stage2_kernel(tr_ref, ti_ref, f_ref, xr_ref, xi_ref):
    n2 = tr_ref.shape[1]
    t = jnp.concatenate([tr_ref[0], ti_ref[0]], axis=0)
    res = _dot(f_ref[...], t)
    xr_ref[0] = res[:n2].astype(BF16)
    xi_ref[0] = res[n2:].astype(BF16)


def _dft_stage2(tr, ti, f2, *, bk=8):
    b, n2, n1, c = tr.shape
    spec = pl.BlockSpec((1, n2, bk * c), lambda i, j: (i, 0, j))
    xr, xi = pl.pallas_call(
        _dft_stage2_kernel,
        grid=(b, n1 // bk),
        in_specs=[spec, spec, pl.BlockSpec((2 * n2, 2 * n2), lambda i, j: (0, 0))],
        out_specs=[spec, spec],
        out_shape=[jax.ShapeDtypeStruct((b, n2, n1 * c), BF16)] * 2,
        compiler_params=_params(2),
        name="dft_stage2",
    )(tr.reshape(b, n2, n1 * c), ti.reshape(b, n2, n1 * c), f2)
    return xr.reshape(b, n2 * n1, c), xi.reshape(b, n2 * n1, c)


def _ret_bwd_kernel(dec_ref, k_ref, v_ref, rb_ref, zb_ref, gc_ref, st_ref, *, cc):
    b = pl.program_id(0)
    t = pl.program_id(1)

    @pl.when((b == 0) & (t == 0))
    def _():
        rowi = lax.broadcasted_iota(jnp.int32, (cc, RET_DV), 0).astype(F32)
        for h in range(RET_HEADS):
            lgb = _log_sigmoid(dec_ref[1, h])
            zb_ref[h] = jnp.exp(lgb * rowi)[:, :RET_DK]
            gc_ref[h] = jnp.exp(lgb * float(cc))

    @pl.when(t == 0)
    def _():
        st_ref[...] = jnp.zeros_like(st_ref)

    for h in range(RET_HEADS):
        st = st_ref[h]
        rb_ref[0, 0, h * RET_DK:(h + 1) * RET_DK, :] = st.astype(BF16)
        kh = k_ref[0, :, h * RET_DK:(h + 1) * RET_DK]
        vh = v_ref[0, :, h * RET_DV:(h + 1) * RET_DV]
        kz = (kh.astype(F32) * zb_ref[h]).astype(BF16)
        st_ref[h] = gc_ref[h] * st + _dot_tn(kz, vh)


def _ret_bwd(dec, k, v, *, cc):
    b, s, _ = k.shape
    nc = s // cc
    return pl.pallas_call(
        functools.partial(_ret_bwd_kernel, cc=cc),
        grid=(b, nc),
        in_specs=[
            pl.BlockSpec(dec.shape, lambda i, t: (0, 0, 0, 0)),
            pl.BlockSpec((1, cc, RET_QK_WIDTH), lambda i, t: (i, nc - 1 - t, 0)),
            pl.BlockSpec((1, cc, RET_V_WIDTH), lambda i, t: (i, nc - 1 - t, 0)),
        ],
        out_specs=pl.BlockSpec((1, 1, RET_QK_WIDTH, RET_DV), lambda i, t: (i, nc - 1 - t, 0, 0)),
        out_shape=jax.ShapeDtypeStruct((b, nc, RET_QK_WIDTH, RET_DV), BF16),
        scratch_shapes=[
            pltpu.VMEM((RET_HEADS, cc, RET_DK), F32),
            pltpu.VMEM((RET_HEADS, 1, RET_DV), F32),
            pltpu.VMEM((RET_HEADS, RET_DK, RET_DV), F32),
        ],
        compiler_params=_params(2),
        name="ret_bwd",
    )(dec, k, v)


def _ret_fwd_kernel(dec_ref, q_ref, k_ref, v_ref, sg_ref, rb_ref, o_ref,
                    dm_ref, xif_ref, xib_ref, zf_ref, gc_ref, st_ref, *, cc):
    b = pl.program_id(0)
    j = pl.program_id(1)

    @pl.when((b == 0) & (j == 0))
    def _():
        row = lax.broadcasted_iota(jnp.int32, (cc, cc), 0)
        col = lax.broadcasted_iota(jnp.int32, (cc, cc), 1)
        diff = (row - col).astype(F32)
        rowi = lax.broadcasted_iota(jnp.int32, (cc, RET_DV), 0).astype(F32)
        for h in range(RET_HEADS):
            lgf = _log_sigmoid(dec_ref[0, h])
            lgb = _log_sigmoid(dec_ref[1, h])
            dm_ref[h] = jnp.where(diff >= 0.0,
                                  jnp.exp(lgf * jnp.maximum(diff, 0.0)),
                                  jnp.exp(lgb * jnp.maximum(-diff, 0.0)))
            xif_ref[h] = jnp.exp(lgf * (rowi + 1.0))
            xib_ref[h] = jnp.exp(lgb * (float(cc) - rowi))
            zf_ref[h] = jnp.exp(lgf * (float(cc) - 1.0 - rowi))[:, :RET_DK]
            gc_ref[h] = jnp.exp(lgf * float(cc))

    @pl.when(j == 0)
    def _():
        st_ref[...] = jnp.zeros_like(st_ref)

    for h in range(RET_HEADS):
        qh = q_ref[0, :, h * RET_DK:(h + 1) * RET_DK]
        kh = k_ref[0, :, h * RET_DK:(h + 1) * RET_DK]
        vh = v_ref[0, :, h * RET_DV:(h + 1) * RET_DV]
        st = st_ref[h]
        p = (_dot_nt(qh, kh) * dm_ref[h]).astype(BF16)
        y = _dot(p, vh)
        y = y + xif_ref[h] * _dot(qh, st.astype(BF16))
        y = y + xib_ref[h] * _dot(qh, rb_ref[0, 0, h * RET_DK:(h + 1) * RET_DK, :])
        kz = (kh.astype(F32) * zf_ref[h]).astype(BF16)
        st_ref[h] = gc_ref[h] * st + _dot_tn(kz, vh)
        ms = jnp.mean(y * y, axis=-1, keepdims=True)
        yn = y * lax.rsqrt(ms + EPS)
        sg = sg_ref[0, :, h * RET_DV:(h + 1) * RET_DV].astype(F32)
        o_ref[0, :, h * RET_DV:(h + 1) * RET_DV] = (sg * yn).astype(BF16)


def _ret_fwd(dec, q, k, v, sg, rb, *, cc):
    b, s, _ = q.shape
    nc = s // cc

    def tok(width):
        return pl.BlockSpec((1, cc, width), lambda i, j: (i, j, 0))

    return pl.pallas_call(
        functools.partial(_ret_fwd_kernel, cc=cc),
        grid=(b, nc),
        in_specs=[
            pl.BlockSpec(dec.shape, lambda i, j: (0, 0, 0, 0)),
            tok(RET_QK_WIDTH), tok(RET_QK_WIDTH), tok(RET_V_WIDTH), tok(RET_V_WIDTH),
            pl.BlockSpec((1, 1, RET_QK_WIDTH, RET_DV), lambda i, j: (i, j, 0, 0)),
        ],
        out_specs=tok(RET_V_WIDTH),
        out_shape=jax.ShapeDtypeStruct((b, s, RET_V_WIDTH), BF16),
        scratch_shapes=[
            pltpu.VMEM((RET_HEADS, cc, cc), F32),
            pltpu.VMEM((RET_HEADS, cc, RET_DV), F32),
            pltpu.VMEM((RET_HEADS, cc, RET_DV), F32),
            pltpu.VMEM((RET_HEADS, cc, RET_DK), F32),
            pltpu.VMEM((RET_HEADS, 1, RET_DV), F32),
            pltpu.VMEM((RET_HEADS, RET_DK, RET_DV), F32),
        ],
        compiler_params=_params(2),
        name="ret_fwd",
    )(dec, q, k, v, sg, rb)


def _mix_out_kernel(x_ref, xr_ref, xi_ref, yr_ref, gt_ref, bdr_ref, bdi_ref, fw_ref, rw_ref, mw_ref,
                    o_ref, *, scale):
    yf = (_dot(xr_ref[...], bdr_ref[...]) + _dot(xi_ref[...], bdi_ref[...])) * scale
    ya = _dot(yf.astype(BF16), fw_ref[...])
    yb = _dot(yr_ref[...], rw_ref[...])
    d = ya.shape[-1]
    ga = gt_ref[:, :d].astype(F32)
    gb = gt_ref[:, d:].astype(F32)
    merged = (ga * ya + gb * yb).astype(BF16)
    o_ref[...] = x_ref[...] + _dot(merged, mw_ref[...])


def _mix_out(x2, xr, xi, yr, gt, bdr, bdi, fw, rw, mw, *, scale, tm=512):
    t, d = x2.shape

    def tok(width):
        return pl.BlockSpec((tm, width), lambda i: (i, 0))

    def full(a):
        return pl.BlockSpec(a.shape, lambda i: (0, 0))

    return pl.pallas_call(
        functools.partial(_mix_out_kernel, scale=scale),
        grid=(t // tm,),
        in_specs=[tok(d), tok(FOURIER_WIDTH), tok(FOURIER_WIDTH), tok(RET_V_WIDTH), tok(2 * d),
                  full(bdr), full(bdi), full(fw), full(rw), full(mw)],
        out_specs=tok(d),
        out_shape=jax.ShapeDtypeStruct((t, d), F32),
        compiler_params=_params(1),
        name="mix_out",
    )(x2, xr, xi, yr, gt, bdr, bdi, fw, rw, mw)


def _mem_kv_kernel(m_ref, g_ref, w_ref, o_ref):
    mn = _rmsnorm(m_ref[...], g_ref[...]).astype(BF16)
    o_ref[...] = _dot(mn, w_ref[...]).astype(BF16)


def _mem_kv(mem2, gain, w, *, tm=256):
    t, d = mem2.shape
    n = w.shape[1]
    return pl.pallas_call(
        _mem_kv_kernel,
        grid=(t // tm,),
        in_specs=[pl.BlockSpec((tm, d), lambda i: (i, 0)),
                  pl.BlockSpec((1, d), lambda i: (0, 0)),
                  pl.BlockSpec((d, n), lambda i: (0, 0))],
        out_specs=pl.BlockSpec((tm, n), lambda i: (i, 0)),
        out_shape=jax.ShapeDtypeStruct((t, n), BF16),
        compiler_params=_params(1),
        name="mem_kv",
    )(mem2, gain, w)


def _xattn_kernel(x_ref, g_ref, wq_ref, kv_ref, wo_ref, o_ref):
    x = x_ref[0]
    d = x.shape[-1]
    xn = _rmsnorm(x, g_ref[...]).astype(BF16)
    q = _dot(xn, wq_ref[...]).astype(BF16)
    acc = x
    for h in range(XA_HEADS):
        lo, hi = h * XA_HEAD_DIM, (h + 1) * XA_HEAD_DIM
        s = _dot_nt(q[:, lo:hi], kv_ref[0, :, lo:hi]) * (XA_HEAD_DIM ** -0.5)
        e = jnp.exp(s - jnp.max(s, axis=-1, keepdims=True))
        p = e / jnp.sum(e, axis=-1, keepdims=True)
        oh = _dot(p.astype(BF16), kv_ref[0, :, d + lo:d + hi])
        acc = acc + _dot(oh.astype(BF16), wo_ref[lo:hi, :])
    o_ref[0] = acc


def _xattn(x3, gain, wq, kv, wo, *, tm=512):
    b, s, d = x3.shape
    m = kv.shape[1]
    tok = pl.BlockSpec((1, tm, d), lambda i, j: (i, j, 0))
    return pl.pallas_call(
        _xattn_kernel,
        grid=(b, s // tm),
        in_specs=[tok,
                  pl.BlockSpec((1, d), lambda i, j: (0, 0)),
                  pl.BlockSpec((d, d), lambda i, j: (0, 0)),
                  pl.BlockSpec((1, m, 2 * d), lambda i, j: (i, 0, 0)),
                  pl.BlockSpec((d, d), lambda i, j: (0, 0))],
        out_specs=tok,
        out_shape=jax.ShapeDtypeStruct((b, s, d), F32),
        compiler_params=_params(2),
        name="xattn",
    )(x3, gain, wq, kv, wo)


def _angle(num, n):
    return (2.0 * math.pi / n) * (num % n).astype(F32)


def _dft_cos_sin(n):
    i = jnp.arange(n, dtype=jnp.int32)
    ang = _angle(i[:, None] * i[None, :], n)
    return jnp.cos(ang), jnp.sin(ang)


def _rotary_tables(s):
    half = RET_DK // 2
    inv = 1.0 / (ROPE_BASE ** (jnp.arange(half, dtype=F32) * 2.0 / RET_DK))
    ang = jnp.arange(s, dtype=F32)[:, None] * inv[None, :]
    cos, sin = jnp.cos(ang), jnp.sin(ang)
    return jnp.concatenate([cos, cos], -1), jnp.concatenate([-sin, sin], -1)


def _seq_tables(s):
    tabs = {"rot": _rotary_tables(s)}
    if s >= _DFT_TWO_STAGE_MIN:
        n1, n2 = _DFT_N1, s // _DFT_N1
        c1, s1 = _dft_cos_sin(n1)
        tabs["f1"] = jnp.concatenate([c1, -s1], 0).astype(BF16)
        c2, s2 = _dft_cos_sin(n2)
        tabs["f2"] = jnp.concatenate(
            [jnp.concatenate([c2, s2], 1), jnp.concatenate([-s2, c2], 1)], 0).astype(BF16)
        ang = _angle(jnp.arange(n2, dtype=jnp.int32)[:, None] * jnp.arange(n1, dtype=jnp.int32)[None, :], s)
        lane = (n2, n1, FOURIER_GROUP_DIM)
        tabs["tw"] = (jnp.broadcast_to(jnp.cos(ang)[:, :, None], lane),
                      jnp.broadcast_to(jnp.sin(ang)[:, :, None], lane))
    else:
        c, sn = _dft_cos_sin(s)
        tabs["f"] = jnp.concatenate([c, -sn], 0).astype(BF16)
    return tabs


def _channel_dft_blocks():
    c, s = _dft_cos_sin(FOURIER_GROUP_DIM)
    eye = jnp.eye(FOURIER_WIDTH // FOURIER_GROUP_DIM, dtype=F32)
    return jnp.kron(eye, c).astype(BF16), jnp.kron(eye, s).astype(BF16)


def _trunk(x, mem, w, bd):
    b, s, d = x.shape
    m = mem.shape[1]
    tabs = _seq_tables(s)
    cosf, sinf = tabs["rot"]
    scale = float((s * FOURIER_GROUP_DIM) ** -0.5)
    x2 = x.reshape(b * s, d)
    mem2 = mem.reshape(b * m, d)
    for l in range(DEPTH):
        lw = w[l]
        x2 = _ffn(x2, lw["ffn1_norm"], lw["ffn1_wg"], lw["ffn1_wu"], lw["ffn1_wo"], w["final_norm"],
                  final_norm=False)
        hf, q, k, v, sg, gt = _mix_in(x2.reshape(b, s, d), lw["mix_norm"], lw["mix_w_in"], cosf, sinf)
        if "f" in tabs:
            xr, xi = _dft_direct(hf, tabs["f"])
        else:
            tr, ti = _dft_stage1(hf, tabs["f1"], *tabs["tw"], n1=_DFT_N1)
            xr, xi = _dft_stage2(tr, ti, tabs["f2"])
        rb = _ret_bwd(lw["dec"], k, v, cc=_RET_CHUNK)
        yr = _ret_fwd(lw["dec"], q, k, v, sg, rb, cc=_RET_CHUNK)
        x2 = _mix_out(x2, xr.reshape(b * s, -1), xi.reshape(b * s, -1), yr.reshape(b * s, -1),
                      gt.reshape(b * s, -1), bd[0], bd[1], lw["fourier_w"], lw["ret_w_out"],
                      lw["mix_w_out"], scale=scale)
        kv = _mem_kv(mem2, lw["mem_norm"], lw["xa_wkv"]).reshape(b, m, 2 * d)
        x2 = _xattn(x2.reshape(b, s, d), lw["xa_norm"], lw["xa_wq"], kv, lw["xa_wo"]).reshape(b * s, d)
        x2 = _ffn(x2, lw["ffn2_norm"], lw["ffn2_wg"], lw["ffn2_wu"], lw["ffn2_wo"], w["final_norm"],
                  final_norm=(l == DEPTH - 1))
    return x2.reshape(b, s, d)


def kernel(x_prompt, x_sample, mem_prompt, mem_sample, ffn1_norm, ffn1_w_in, ffn1_w_out, mix_norm, mix_w_in, fourier_w, ret_decay_fwd, ret_decay_bwd, ret_w_out, mix_w_out, xa_norm, mem_norm, xa_wq, xa_wkv, xa_wo, ffn2_norm, ffn2_w_in, ffn2_w_out, final_norm):
    bf = lambda a: a.astype(BF16)
    row = lambda a: a.reshape(1, -1)
    w = {"final_norm": row(final_norm)}
    for l in range(DEPTH):
        dec = jnp.stack([ret_decay_fwd[l], ret_decay_bwd[l]])
        w[l] = {
            "ffn1_norm": row(ffn1_norm[l]),
            "ffn1_wg": bf(ffn1_w_in[l][:, :D_FF]), "ffn1_wu": bf(ffn1_w_in[l][:, D_FF:]),
            "ffn1_wo": bf(ffn1_w_out[l]),
            "mix_norm": row(mix_norm[l]), "mix_w_in": bf(mix_w_in[l]),
            "fourier_w": bf(fourier_w[l]), "ret_w_out": bf(ret_w_out[l]), "mix_w_out": bf(mix_w_out[l]),
            "dec": jnp.broadcast_to(dec[:, :, None, None], (2, RET_HEADS, 1, RET_DV)),
            "xa_norm": row(xa_norm[l]), "mem_norm": row(mem_norm[l]),
            "xa_wq": bf(xa_wq[l]), "xa_wkv": bf(xa_wkv[l]), "xa_wo": bf(xa_wo[l]),
            "ffn2_norm": row(ffn2_norm[l]),
            "ffn2_wg": bf(ffn2_w_in[l][:, :D_FF]), "ffn2_wu": bf(ffn2_w_in[l][:, D_FF:]),
            "ffn2_wo": bf(ffn2_w_out[l]),
        }
    bd = _channel_dft_blocks()
    y_prompt = _trunk(x_prompt, mem_prompt, w, bd)
    y_sample = _trunk(x_sample, mem_sample, w, bd)
    return (y_prompt, y_sample)
```

```python
import functools
import math

import jax
import jax.numpy as jnp
from jax import lax
from jax.experimental import pallas as pl
from jax.experimental.pallas import tpu as pltpu

F32 = jnp.float32
BF16 = jnp.bfloat16

D_MODEL = 1024
DEPTH = 4
D_FF = 2816
FOURIER_GROUP_DIM = 128
FOURIER_WIDTH = 512
RET_HEADS = 4
RET_DK = 128
RET_DV = 256
RET_QK_WIDTH = RET_HEADS * RET_DK
RET_V_WIDTH = RET_HEADS * RET_DV
XA_HEADS = 4
XA_HEAD_DIM = D_MODEL // XA_HEADS
ROPE_BASE = 10000.0
EPS = 1e-6

_C_F = (0, FOURIER_WIDTH)
_C_Q = (_C_F[1], _C_F[1] + RET_QK_WIDTH)
_C_K = (_C_Q[1], _C_Q[1] + RET_QK_WIDTH)
_C_V = (_C_K[1], _C_K[1] + RET_V_WIDTH)
_C_G = (_C_V[1], _C_V[1] + RET_V_WIDTH)
_C_GATES = (_C_G[1], _C_G[1] + 2 * D_MODEL)

_V7X_VMEM_BYTES = 64 * 1024 * 1024
_VMEM_LIMIT = _V7X_VMEM_BYTES - 8 * 1024 * 1024
_V7X_LANES = 128
_V7X_SUBLANES = 8

_RET_CHUNK = 256
_DFT_TWO_STAGE_MIN = 16384
_DFT_N1 = 128


def _params(n_axes):
    return pltpu.CompilerParams(
        dimension_semantics=("arbitrary",) * n_axes, vmem_limit_bytes=_VMEM_LIMIT)


def _resident(shape):
    nd = len(shape)
    return pl.BlockSpec(shape, lambda *_: (0,) * nd, pipeline_mode=pl.Buffered(1))


def _rmsnorm(x, g):
    ms = jnp.mean(x * x, axis=-1, keepdims=True)
    return x * lax.rsqrt(ms + EPS) * g


def _sigmoid(x):
    return 1.0 / (1.0 + jnp.exp(-x))


def _log_sigmoid(x):
    return -(jnp.maximum(-x, 0.0) + jnp.log1p(jnp.exp(-jnp.abs(x))))


def _dot(a, b):
    return jnp.dot(a, b, preferred_element_type=F32)


def _dot_nt(a, b):
    return lax.dot_general(a, b, (((1,), (1,)), ((), ())), preferred_element_type=F32)


def _dot_tn(a, b):
    return lax.dot_general(a, b, (((0,), (0,)), ((), ())), preferred_element_type=F32)


def _row_slices(total, rows):
    return [slice(r * rows, (r + 1) * rows) for r in range(total // rows)]


def _ffn_kernel(x_ref, g_ref, wg_ref, wu_ref, wo_ref, fg_ref, o_ref, *, rows, tf, final_norm):
    f = wg_ref.shape[1]
    for sl in _row_slices(x_ref.shape[0], rows):
        x = x_ref[sl, :]
        xn = _rmsnorm(x, g_ref[...]).astype(BF16)
        acc = None
        for c in range(f // tf):
            cs = slice(c * tf, (c + 1) * tf)
            gate = _dot(xn, wg_ref[:, cs])
            up = _dot(xn, wu_ref[:, cs])
            act = (gate * _sigmoid(gate) * up).astype(BF16)
            part = _dot(act, wo_ref[cs, :])
            acc = part if acc is None else acc + part
        y = x + 0.5 * acc
        if final_norm:
            y = _rmsnorm(y, fg_ref[...])
        o_ref[sl, :] = y


def _ffn(x2, gain, wg, wu, wo, final_gain, *, final_norm, tm=512, rows=256, tf=1408):
    t, d = x2.shape
    f = wg.shape[1]
    return pl.pallas_call(
        functools.partial(_ffn_kernel, rows=rows, tf=tf, final_norm=final_norm),
        grid=(t // tm,),
        in_specs=[
            pl.BlockSpec((tm, d), lambda i: (i, 0)),
            _resident((1, d)), _resident((d, f)), _resident((d, f)), _resident((f, d)), _resident((1, d)),
        ],
        out_specs=pl.BlockSpec((tm, d), lambda i: (i, 0)),
        out_shape=jax.ShapeDtypeStruct((t, d), F32),
        compiler_params=_params(1),
        name="ffn",
    )(x2, gain, wg, wu, wo, final_gain)


def _mix_in_kernel(x_ref, g_ref, w_ref, cos_ref, sin_ref,
                   hf_ref, q_ref, k_ref, v_ref, sg_ref, gt_ref, *, rows):
    def rotary(hx, out_ref, sl, scale):
        cosf = cos_ref[sl, :]
        sinf = sin_ref[sl, :]
        for h in range(RET_HEADS):
            blk = hx[:, h * RET_DK:(h + 1) * RET_DK]
            rot = blk * cosf + pltpu.roll(blk, RET_DK // 2, 1) * sinf
            if scale != 1.0:
                rot = rot * scale
            out_ref[0, sl, h * RET_DK:(h + 1) * RET_DK] = rot.astype(out_ref.dtype)

    for sl in _row_slices(x_ref.shape[1], rows):
        xn = _rmsnorm(x_ref[0, sl, :], g_ref[...]).astype(BF16)

        def proj(cols):
            return _dot(xn, w_ref[:, cols[0]:cols[1]])

        hf_ref[0, sl, :] = proj(_C_F).astype(hf_ref.dtype)
        rotary(proj(_C_Q), q_ref, sl, 1.0)
        rotary(proj(_C_K), k_ref, sl, RET_DK ** -0.5)
        v_ref[0, sl, :] = proj(_C_V).astype(v_ref.dtype)
        hg = proj(_C_G)
        sg_ref[0, sl, :] = (hg * _sigmoid(hg)).astype(sg_ref.dtype)
        gt_ref[0, sl, :] = _sigmoid(proj(_C_GATES)).astype(gt_ref.dtype)


def _mix_in(x3, gain, w, cosf, sinf, *, hf_dtype, tm=512, rows=256):
    b, s, d = x3.shape
    n = w.shape[1]

    def tok(width):
        return pl.BlockSpec((1, tm, width), lambda i, j: (i, j, 0))

    def out(width, dtype=BF16):
        return jax.ShapeDtypeStruct((b, s, width), dtype)

    return pl.pallas_call(
        functools.partial(_mix_in_kernel, rows=rows),
        grid=(b, s // tm),
        in_specs=[
            tok(d), _resident((1, d)), _resident((d, n)),
            pl.BlockSpec((tm, RET_DK), lambda i, j: (j, 0)),
            pl.BlockSpec((tm, RET_DK), lambda i, j: (j, 0)),
        ],
        out_specs=[tok(FOURIER_WIDTH), tok(RET_QK_WIDTH), tok(RET_QK_WIDTH),
                   tok(RET_V_WIDTH), tok(RET_V_WIDTH), tok(2 * D_MODEL)],
        out_shape=[out(FOURIER_WIDTH, hf_dtype), out(RET_QK_WIDTH), out(RET_QK_WIDTH),
                   out(RET_V_WIDTH), out(RET_V_WIDTH, F32), out(2 * D_MODEL)],
        compiler_params=_params(2),
        name="mix_in",
    )(x3, gain, w, cosf, sinf)


def _dft_direct_kernel(a_ref, f_ref, xr_ref, xi_ref):
    s = a_ref.shape[1]
    res = _dot(f_ref[...], a_ref[0])
    xr_ref[0] = res[:s].astype(BF16)
    xi_ref[0] = res[s:].astype(BF16)


def _dft_direct(hf, fmat, *, bw=256):
    b, s, c = hf.shape
    spec = pl.BlockSpec((1, s, bw), lambda i, j: (i, 0, j))
    return pl.pallas_call(
        _dft_direct_kernel,
        grid=(b, c // bw),
        in_specs=[spec, _resident((2 * s, s))],
        out_specs=[spec, spec],
        out_shape=[jax.ShapeDtypeStruct((b, s, c), BF16)] * 2,
        compiler_params=_params(2),
        name="dft_direct",
    )(hf, fmat)


def _dft_stage1_kernel(a_ref, f_ref, tc_ref, ts_ref, tr_ref, ti_ref, a2, tr2, ti2):
    n1 = a_ref.shape[1]
    blk = (n1, _V7X_SUBLANES, _V7X_LANES)
    a2[...] = a_ref[0].reshape(a2.shape)
    for j in range(_V7X_SUBLANES):
        pick = pl.ds(j, n1, stride=_V7X_SUBLANES)
        res = _dot(f_ref[...], a2[pick, :].astype(BF16))
        ur, ui = res[:n1], res[n1:]
        cw, sw = tc_ref[j], ts_ref[j]
        tr2[pick, :] = ur * cw + ui * sw
        ti2[pick, :] = ui * cw - ur * sw
    tr_ref[0] = tr2[...].reshape(blk)
    ti_ref[0] = ti2[...].reshape(blk)


def _dft_stage1(hf, f1, twc, tws, *, n1):
    b, s, c = hf.shape
    n2 = s // n1
    blk = pl.BlockSpec((1, n1, _V7X_SUBLANES, _V7X_LANES), lambda i, j, k: (i, 0, j, k))
    tw = pl.BlockSpec((_V7X_SUBLANES, n1, _V7X_LANES), lambda i, j, k: (j, 0, 0))
    rows = pltpu.VMEM((n1 * _V7X_SUBLANES, _V7X_LANES), F32)
    return pl.pallas_call(
        _dft_stage1_kernel,
        grid=(b, n2 // _V7X_SUBLANES, c // _V7X_LANES),
        in_specs=[blk, _resident((2 * n1, n1)), tw, tw],
        out_specs=[blk, blk],
        out_shape=[jax.ShapeDtypeStruct((b, n1, n2, c), F32)] * 2,
        scratch_shapes=[rows, rows, rows],
        compiler_params=_params(3),
        name="dft_stage1",
    )(hf.reshape(b, n1, n2, c), f1, twc, tws)


def _dft_stage2_kernel(tr_ref, ti_ref, f_ref, xr_ref, xi_ref, xr2, xi2):
    n2 = tr_ref.shape[2]
    blk = (n2, _V7X_SUBLANES, _V7X_LANES)
    for j in range(_V7X_SUBLANES):
        t = jnp.concatenate([tr_ref[0, j], ti_ref[0, j]], axis=0).astype(BF16)
        res = _dot(f_ref[...], t)
        pick = pl.ds(j, n2, stride=_V7X_SUBLANES)
        xr2[pick, :] = res[:n2]
        xi2[pick, :] = res[n2:]
    xr_ref[0] = xr2[...].reshape(blk)
    xi_ref[0] = xi2[...].reshape(blk)


def _dft_stage2(tr, ti, f2):
    b, n1, n2, c = tr.shape
    inb = pl.BlockSpec((1, _V7X_SUBLANES, n2, _V7X_LANES), lambda i, j, k: (i, j, 0, k))
    outb = pl.BlockSpec((1, n2, _V7X_SUBLANES, _V7X_LANES), lambda i, j, k: (i, 0, j, k))
    rows = pltpu.VMEM((n2 * _V7X_SUBLANES, _V7X_LANES), F32)
    xr, xi = pl.pallas_call(
        _dft_stage2_kernel,
        grid=(b, n1 // _V7X_SUBLANES, c // _V7X_LANES),
        in_specs=[inb, inb, _resident((2 * n2, 2 * n2))],
        out_specs=[outb, outb],
        out_shape=[jax.ShapeDtypeStruct((b, n2, n1, c), F32)] * 2,
        scratch_shapes=[rows, rows],
        compiler_params=_params(3),
        name="dft_stage2",
    )(tr, ti, f2)
    return xr.reshape(b, n2 * n1, c), xi.reshape(b, n2 * n1, c)


def _ret_bwd_kernel(dec_ref, k_ref, v_ref, rb_ref, zb_ref, gc_ref, st_ref, *, cc, nch):
    b = pl.program_id(0)
    t = pl.program_id(1)

    @pl.when((b == 0) & (t == 0))
    def _():
        rowi = lax.broadcasted_iota(jnp.int32, (cc, RET_DV), 0).astype(F32)
        for h in range(RET_HEADS):
            lgb = _log_sigmoid(dec_ref[1, h])
            zb_ref[h] = jnp.exp(lgb * rowi)[:, :RET_DK]
            gc_ref[h] = jnp.exp(lgb * float(cc))

    @pl.when(t == 0)
    def _():
        st_ref[...] = jnp.zeros_like(st_ref)

    for h in range(RET_HEADS):
        st = st_ref[h]
        for c in reversed(range(nch)):
            rs = slice(c * cc, (c + 1) * cc)
            rb_ref[0, c, h * RET_DK:(h + 1) * RET_DK, :] = st.astype(BF16)
            kh = k_ref[0, rs, h * RET_DK:(h + 1) * RET_DK]
            vh = v_ref[0, rs, h * RET_DV:(h + 1) * RET_DV]
            kz = (kh.astype(F32) * zb_ref[h]).astype(BF16)
            st = gc_ref[h] * st + _dot_tn(kz, vh)
        st_ref[h] = st


def _ret_bwd(dec, k, v, *, cc, nch=4):
    b, s, _ = k.shape
    ns = s // (cc * nch)
    return pl.pallas_call(
        functools.partial(_ret_bwd_kernel, cc=cc, nch=nch),
        grid=(b, ns),
        in_specs=[
            _resident(dec.shape),
            pl.BlockSpec((1, cc * nch, RET_QK_WIDTH), lambda i, t: (i, ns - 1 - t, 0)),
            pl.BlockSpec((1, cc * nch, RET_V_WIDTH), lambda i, t: (i, ns - 1 - t, 0)),
        ],
        out_specs=pl.BlockSpec((1, nch, RET_QK_WIDTH, RET_DV), lambda i, t: (i, ns - 1 - t, 0, 0)),
        out_shape=jax.ShapeDtypeStruct((b, s // cc, RET_QK_WIDTH, RET_DV), BF16),
        scratch_shapes=[
            pltpu.VMEM((RET_HEADS, cc, RET_DK), F32),
            pltpu.VMEM((RET_HEADS, 1, RET_DV), F32),
            pltpu.VMEM((RET_HEADS, RET_DK, RET_DV), F32),
        ],
        compiler_params=_params(2),
        name="ret_bwd",
    )(dec, k, v)


def _ret_fwd_kernel(dec_ref, q_ref, k_ref, v_ref, sg_ref, rb_ref, o_ref,
                    dm_ref, xif_ref, xib_ref, zf_ref, gc_ref, st_ref, *, cc, nch):
    b = pl.program_id(0)
    j = pl.program_id(1)

    @pl.when((b == 0) & (j == 0))
    def _():
        row = lax.broadcasted_iota(jnp.int32, (cc, cc), 0)
        col = lax.broadcasted_iota(jnp.int32, (cc, cc), 1)
        diff = (row - col).astype(F32)
        rowi = lax.broadcasted_iota(jnp.int32, (cc, RET_DV), 0).astype(F32)
        for h in range(RET_HEADS):
            lgf = _log_sigmoid(dec_ref[0, h])
            lgb = _log_sigmoid(dec_ref[1, h])
            dm_ref[h] = jnp.where(diff >= 0.0,
                                  jnp.exp(lgf * jnp.maximum(diff, 0.0)),
                                  jnp.exp(lgb * jnp.maximum(-diff, 0.0)))
            xif_ref[h] = jnp.exp(lgf * (rowi + 1.0))[:, :RET_DK]
            xib_ref[h] = jnp.exp(lgb * (float(cc) - rowi))[:, :RET_DK]
            zf_ref[h] = jnp.exp(lgf * (float(cc) - 1.0 - rowi))[:, :RET_DK]
            gc_ref[h] = jnp.exp(lgf * float(cc))

    @pl.when(j == 0)
    def _():
        st_ref[...] = jnp.zeros_like(st_ref)

    for h in range(RET_HEADS):
        st = st_ref[h]
        for c in range(nch):
            rs = slice(c * cc, (c + 1) * cc)
            qh = q_ref[0, rs, h * RET_DK:(h + 1) * RET_DK]
            kh = k_ref[0, rs, h * RET_DK:(h + 1) * RET_DK]
            vh = v_ref[0, rs, h * RET_DV:(h + 1) * RET_DV]
            p = (_dot_nt(qh, kh) * dm_ref[h]).astype(BF16)
            qf32 = qh.astype(F32)
            qf = (qf32 * xif_ref[h]).astype(BF16)
            qb = (qf32 * xib_ref[h]).astype(BF16)
            lhs = jnp.concatenate([p, qf, qb], axis=1)
            rhs = jnp.concatenate(
                [vh, st.astype(BF16), rb_ref[0, c, h * RET_DK:(h + 1) * RET_DK, :]], axis=0)
            y = _dot(lhs, rhs)
            kz = (kh.astype(F32) * zf_ref[h]).astype(BF16)
            st = gc_ref[h] * st + _dot_tn(kz, vh)
            ms = jnp.mean(y * y, axis=-1, keepdims=True)
            yn = y * lax.rsqrt(ms + EPS)
            sg = sg_ref[0, rs, h * RET_DV:(h + 1) * RET_DV]
            o_ref[0, rs, h * RET_DV:(h + 1) * RET_DV] = (sg * yn).astype(BF16)
        st_ref[h] = st


def _ret_fwd(dec, q, k, v, sg, rb, *, cc, nch=2):
    b, s, _ = q.shape
    ns = s // (cc * nch)

    def tok(width):
        return pl.BlockSpec((1, cc * nch, width), lambda i, j: (i, j, 0))

    return pl.pallas_call(
        functools.partial(_ret_fwd_kernel, cc=cc, nch=nch),
        grid=(b, ns),
        in_specs=[
            _resident(dec.shape),
            tok(RET_QK_WIDTH), tok(RET_QK_WIDTH), tok(RET_V_WIDTH), tok(RET_V_WIDTH),
            pl.BlockSpec((1, nch, RET_QK_WIDTH, RET_DV), lambda i, j: (i, j, 0, 0)),
        ],
        out_specs=tok(RET_V_WIDTH),
        out_shape=jax.ShapeDtypeStruct((b, s, RET_V_WIDTH), BF16),
        scratch_shapes=[
            pltpu.VMEM((RET_HEADS, cc, cc), F32),
            pltpu.VMEM((RET_HEADS, cc, RET_DK), F32),
            pltpu.VMEM((RET_HEADS, cc, RET_DK), F32),
            pltpu.VMEM((RET_HEADS, cc, RET_DK), F32),
            pltpu.VMEM((RET_HEADS, 1, RET_DV), F32),
            pltpu.VMEM((RET_HEADS, RET_DK, RET_DV), F32),
        ],
        compiler_params=_params(2),
        name="ret_fwd",
    )(dec, q, k, v, sg, rb)


def _mix_out_kernel(x_ref, xr_ref, xi_ref, yr_ref, gt_ref, bdr_ref, bdi_ref, fw_ref, rw_ref, mw_ref,
                    o_ref, *, scale, rows):
    d = x_ref.shape[-1]
    for sl in _row_slices(x_ref.shape[0], rows):
        yf = (_dot(xr_ref[sl, :].astype(BF16), bdr_ref[...])
              + _dot(xi_ref[sl, :].astype(BF16), bdi_ref[...])) * scale
        ya = _dot(yf.astype(BF16), fw_ref[...])
        yb = _dot(yr_ref[sl, :], rw_ref[...])
        ga = gt_ref[sl, :d].astype(F32)
        gb = gt_ref[sl, d:].astype(F32)
        merged = (ga * ya + gb * yb).astype(BF16)
        o_ref[sl, :] = x_ref[sl, :] + _dot(merged, mw_ref[...])


def _mix_out(x2, xr, xi, yr, gt, bdr, bdi, fw, rw, mw, *, scale, tm=512, rows=512):
    t, d = x2.shape

    def tok(width):
        return pl.BlockSpec((tm, width), lambda i: (i, 0))

    return pl.pallas_call(
        functools.partial(_mix_out_kernel, scale=scale, rows=rows),
        grid=(t // tm,),
        in_specs=[tok(d), tok(FOURIER_WIDTH), tok(FOURIER_WIDTH), tok(RET_V_WIDTH), tok(2 * d),
                  _resident(bdr.shape), _resident(bdi.shape), _resident(fw.shape),
                  _resident(rw.shape), _resident(mw.shape)],
        out_specs=tok(d),
        out_shape=jax.ShapeDtypeStruct((t, d), F32),
        compiler_params=_params(1),
        name="mix_out",
    )(x2, xr, xi, yr, gt, bdr, bdi, fw, rw, mw)


def _mem_kv_kernel(m_ref, g_ref, w_ref, o_ref):
    mn = _rmsnorm(m_ref[...], g_ref[...]).astype(BF16)
    o_ref[...] = _dot(mn, w_ref[...]).astype(BF16)


def _mem_kv(mem2, gain, w, *, tm=256):
    t, d = mem2.shape
    n = w.shape[1]
    return pl.pallas_call(
        _mem_kv_kernel,
        grid=(t // tm,),
        in_specs=[pl.BlockSpec((tm, d), lambda i: (i, 0)), _resident((1, d)), _resident((d, n))],
        out_specs=pl.BlockSpec((tm, n), lambda i: (i, 0)),
        out_shape=jax.ShapeDtypeStruct((t, n), BF16),
        compiler_params=_params(1),
        name="mem_kv",
    )(mem2, gain, w)


def _xattn_kernel(x_ref, g_ref, wq_ref, kv_ref, wo_ref, o_ref):
    x = x_ref[0]
    d = x.shape[-1]
    heads = [(h * XA_HEAD_DIM, (h + 1) * XA_HEAD_DIM) for h in range(XA_HEADS)]
    xn = _rmsnorm(x, g_ref[...]).astype(BF16)
    q = _dot(xn, wq_ref[...]).astype(BF16)
    scores = [_dot_nt(q[:, lo:hi], kv_ref[0, :, lo:hi]) * (XA_HEAD_DIM ** -0.5) for lo, hi in heads]
    probs = []
    for s in scores:
        e = jnp.exp(s - jnp.max(s, axis=-1, keepdims=True))
        probs.append((e / jnp.sum(e, axis=-1, keepdims=True)).astype(BF16))
    outs = [_dot(p, kv_ref[0, :, d + lo:d + hi]).astype(BF16) for p, (lo, hi) in zip(probs, heads)]
    o_ref[0] = x + _dot(jnp.concatenate(outs, axis=-1), wo_ref[...])


def _xattn(x3, gain, wq, kv, wo, *, tm=512):
    b, s, d = x3.shape
    m = kv.shape[1]
    tok = pl.BlockSpec((1, tm, d), lambda i, j: (i, j, 0))
    return pl.pallas_call(
        _xattn_kernel,
        grid=(b, s // tm),
        in_specs=[tok, _resident((1, d)), _resident((d, d)),
                  pl.BlockSpec((1, m, 2 * d), lambda i, j: (i, 0, 0)),
                  _resident((d, d))],
        out_specs=tok,
        out_shape=jax.ShapeDtypeStruct((b, s, d), F32),
        compiler_params=_params(2),
        name="xattn",
    )(x3, gain, wq, kv, wo)


def _angle(num, n):
    return (2.0 * math.pi / n) * (num % n).astype(F32)


def _dft_cos_sin(n):
    i = jnp.arange(n, dtype=jnp.int32)
    ang = _angle(i[:, None] * i[None, :], n)
    return jnp.cos(ang), jnp.sin(ang)


def _rotary_tables(s):
    half = RET_DK // 2
    inv = 1.0 / (ROPE_BASE ** (jnp.arange(half, dtype=F32) * 2.0 / RET_DK))
    ang = jnp.arange(s, dtype=F32)[:, None] * inv[None, :]
    cos, sin = jnp.cos(ang), jnp.sin(ang)
    return jnp.concatenate([cos, cos], -1), jnp.concatenate([-sin, sin], -1)


def _seq_tables(s):
    tabs = {"rot": _rotary_tables(s)}
    if s >= _DFT_TWO_STAGE_MIN:
        n1, n2 = _DFT_N1, s // _DFT_N1
        c1, s1 = _dft_cos_sin(n1)
        tabs["f1"] = jnp.concatenate([c1, -s1], 0).astype(BF16)
        c2, s2 = _dft_cos_sin(n2)
        tabs["f2"] = jnp.concatenate(
            [jnp.concatenate([c2, s2], 1), jnp.concatenate([-s2, c2], 1)], 0).astype(BF16)
        ang = _angle(jnp.arange(n2, dtype=jnp.int32)[:, None] * jnp.arange(n1, dtype=jnp.int32)[None, :], s)
        lane = (n2, n1, _V7X_LANES)
        tabs["tw"] = (jnp.broadcast_to(jnp.cos(ang)[:, :, None], lane),
                      jnp.broadcast_to(jnp.sin(ang)[:, :, None], lane))
    else:
        c, sn = _dft_cos_sin(s)
        tabs["f"] = jnp.concatenate([c, -sn], 0).astype(BF16)
    return tabs


def _channel_dft_blocks():
    c, s = _dft_cos_sin(FOURIER_GROUP_DIM)
    eye = jnp.eye(FOURIER_WIDTH // FOURIER_GROUP_DIM, dtype=F32)
    return jnp.kron(eye, c).astype(BF16), jnp.kron(eye, s).astype(BF16)


def _trunk(x, mem, w, bd):
    b, s, d = x.shape
    m = mem.shape[1]
    tabs = _seq_tables(s)
    cosf, sinf = tabs["rot"]
    two_stage = "f" not in tabs
    scale = float((s * FOURIER_GROUP_DIM) ** -0.5)
    x2 = x.reshape(b * s, d)
    mem2 = mem.reshape(b * m, d)
    for l in range(DEPTH):
        lw = w[l]
        x2 = _ffn(x2, lw["ffn1_norm"], lw["ffn1_wg"], lw["ffn1_wu"], lw["ffn1_wo"], w["final_norm"],
                  final_norm=False)
        hf, q, k, v, sg, gt = _mix_in(x2.reshape(b, s, d), lw["mix_norm"], lw["mix_w_in"], cosf, sinf,
                                      hf_dtype=F32 if two_stage else BF16)
        if two_stage:
            tr, ti = _dft_stage1(hf, tabs["f1"], *tabs["tw"], n1=_DFT_N1)
            xr, xi = _dft_stage2(tr, ti, tabs["f2"])
        else:
            xr, xi = _dft_direct(hf, tabs["f"])
        rb = _ret_bwd(lw["dec"], k, v, cc=_RET_CHUNK)
        yr = _ret_fwd(lw["dec"], q, k, v, sg, rb, cc=_RET_CHUNK)
        x2 = _mix_out(x2, xr.reshape(b * s, -1), xi.reshape(b * s, -1), yr.reshape(b * s, -1),
                      gt.reshape(b * s, -1), bd[0], bd[1], lw["fourier_w"], lw["ret_w_out"],
                      lw["mix_w_out"], scale=scale)
        kv = _mem_kv(mem2, lw["mem_norm"], lw["xa_wkv"]).reshape(b, m, 2 * d)
        x2 = _xattn(x2.reshape(b, s, d), lw["xa_norm"], lw["xa_wq"], kv, lw["xa_wo"]).reshape(b * s, d)
        x2 = _ffn(x2, lw["ffn2_norm"], lw["ffn2_wg"], lw["ffn2_wu"], lw["ffn2_wo"], w["final_norm"],
                  final_norm=(l == DEPTH - 1))
    return x2.reshape(b, s, d)


def kernel(x_prompt, x_sample, mem_prompt, mem_sample, ffn1_norm, ffn1_w_in, ffn1_w_out, mix_norm, mix_w_in, fourier_w, ret_decay_fwd, ret_decay_bwd, ret_w_out, mix_w_out, xa_norm, mem_norm, xa_wq, xa_wkv, xa_wo, ffn2_norm, ffn2_w_in, ffn2_w_out, final_norm):
    bf = lambda a: a.astype(BF16)
    row = lambda a: a.reshape(1, -1)
    w = {"final_norm": row(final_norm)}
    for l in range(DEPTH):
        dec = jnp.stack([ret_decay_fwd[l], ret_decay_bwd[l]])
        w[l] = {
            "ffn1_norm": row(ffn1_norm[l]),
            "ffn1_wg": bf(ffn1_w_in[l][:, :D_FF]), "ffn1_wu": bf(ffn1_w_in[l][:, D_FF:]),
            "ffn1_wo": bf(ffn1_w_out[l]),
            "mix_norm": row(mix_norm[l]), "mix_w_in": bf(mix_w_in[l]),
            "fourier_w": bf(fourier_w[l]), "ret_w_out": bf(ret_w_out[l]), "mix_w_out": bf(mix_w_out[l]),
            "dec": jnp.broadcast_to(dec[:, :, None, None], (2, RET_HEADS, 1, RET_DV)),
            "xa_norm": row(xa_norm[l]), "mem_norm": row(mem_norm[l]),
            "xa_wq": bf(xa_wq[l]), "xa_wkv": bf(xa_wkv[l]), "xa_wo": bf(xa_wo[l]),
            "ffn2_norm": row(ffn2_norm[l]),
            "ffn2_wg": bf(ffn2_w_in[l][:, :D_FF]), "ffn2_wu": bf(ffn2_w_in[l][:, D_FF:]),
            "ffn2_wo": bf(ffn2_w_out[l]),
        }
    bd = _channel_dft_blocks()
    y_prompt = _trunk(x_prompt, mem_prompt, w, bd)
    y_sample = _trunk(x_sample, mem_sample, w, bd)
    return (y_prompt, y_sample)
```

```python
import functools
import math

import jax
import jax.numpy as jnp
from jax import lax
from jax.experimental import pallas as pl
from jax.experimental.pallas import tpu as pltpu

F32 = jnp.float32
BF16 = jnp.bfloat16

D_MODEL = 1024
DEPTH = 4
D_FF = 2816
FOURIER_GROUP_DIM = 128
FOURIER_WIDTH = 512
RET_HEADS = 4
RET_DK = 128
RET_DV = 256
RET_QK_WIDTH = RET_HEADS * RET_DK
RET_V_WIDTH = RET_HEADS * RET_DV
XA_HEADS = 4
XA_HEAD_DIM = D_MODEL // XA_HEADS
ROPE_BASE = 10000.0
EPS = 1e-6

_C_F = (0, FOURIER_WIDTH)
_C_Q = (_C_F[1], _C_F[1] + RET_QK_WIDTH)
_C_K = (_C_Q[1], _C_Q[1] + RET_QK_WIDTH)
_C_V = (_C_K[1], _C_K[1] + RET_V_WIDTH)
_C_G = (_C_V[1], _C_V[1] + RET_V_WIDTH)
_C_GATES = (_C_G[1], _C_G[1] + 2 * D_MODEL)

_V7X_VMEM_BYTES = 64 * 1024 * 1024
_VMEM_LIMIT = _V7X_VMEM_BYTES - 8 * 1024 * 1024
_V7X_LANES = 128
_V7X_SUBLANES = 8

_RET_CHUNK = 256
_DFT_TWO_STAGE_MIN = 16384
_DFT_N1 = 128


def _params(n_axes):
    return pltpu.CompilerParams(
        dimension_semantics=("arbitrary",) * n_axes, vmem_limit_bytes=_VMEM_LIMIT)


def _resident(shape):
    nd = len(shape)
    return pl.BlockSpec(shape, lambda *_: (0,) * nd, pipeline_mode=pl.Buffered(1))


def _rmsnorm(x, g):
    ms = jnp.mean(x * x, axis=-1, keepdims=True)
    return x * lax.rsqrt(ms + EPS) * g


def _sigmoid(x):
    return 1.0 / (1.0 + jnp.exp(-x))


def _log_sigmoid(x):
    return -(jnp.maximum(-x, 0.0) + jnp.log1p(jnp.exp(-jnp.abs(x))))


def _dot(a, b):
    return jnp.dot(a, b, preferred_element_type=F32)


def _dot_nt(a, b):
    return lax.dot_general(a, b, (((1,), (1,)), ((), ())), preferred_element_type=F32)


def _dot_tn(a, b):
    return lax.dot_general(a, b, (((0,), (0,)), ((), ())), preferred_element_type=F32)


def _row_slices(total, rows):
    return [slice(r * rows, (r + 1) * rows) for r in range(total // rows)]


def _ffn_kernel(x_ref, g_ref, wg_ref, wu_ref, wo_ref, fg_ref, o_ref, *, rows, tf, final_norm):
    f = wg_ref.shape[1]
    for sl in _row_slices(x_ref.shape[0], rows):
        x = x_ref[sl, :]
        xn = _rmsnorm(x, g_ref[...]).astype(BF16)
        acc = None
        for c in range(f // tf):
            cs = slice(c * tf, (c + 1) * tf)
            gate = _dot(xn, wg_ref[:, cs])
            up = _dot(xn, wu_ref[:, cs])
            act = (gate * _sigmoid(gate) * up).astype(BF16)
            part = _dot(act, wo_ref[cs, :])
            acc = part if acc is None else acc + part
        y = x + 0.5 * acc
        if final_norm:
            y = _rmsnorm(y, fg_ref[...])
        o_ref[sl, :] = y


def _ffn(x2, gain, wg, wu, wo, final_gain, *, final_norm, tm=1024, rows=512, tf=1408):
    t, d = x2.shape
    f = wg.shape[1]
    return pl.pallas_call(
        functools.partial(_ffn_kernel, rows=rows, tf=tf, final_norm=final_norm),
        grid=(t // tm,),
        in_specs=[
            pl.BlockSpec((tm, d), lambda i: (i, 0)),
            _resident((1, d)), _resident((d, f)), _resident((d, f)), _resident((f, d)), _resident((1, d)),
        ],
        out_specs=pl.BlockSpec((tm, d), lambda i: (i, 0)),
        out_shape=jax.ShapeDtypeStruct((t, d), F32),
        compiler_params=_params(1),
        name="ffn",
    )(x2, gain, wg, wu, wo, final_gain)


def _mix_in_kernel(x_ref, g_ref, w_ref, cos_ref, sin_ref,
                   hf_ref, q_ref, k_ref, v_ref, sg_ref, gt_ref, *, rows):
    def rotary(hx, out_ref, sl, scale):
        cosf = cos_ref[sl, :]
        sinf = sin_ref[sl, :]
        for h in range(RET_HEADS):
            blk = hx[:, h * RET_DK:(h + 1) * RET_DK]
            rot = blk * cosf + pltpu.roll(blk, RET_DK // 2, 1) * sinf
            if scale != 1.0:
                rot = rot * scale
            out_ref[0, sl, h * RET_DK:(h + 1) * RET_DK] = rot.astype(out_ref.dtype)

    for sl in _row_slices(x_ref.shape[1], rows):
        xn = _rmsnorm(x_ref[0, sl, :], g_ref[...]).astype(BF16)

        def proj(cols):
            return _dot(xn, w_ref[:, cols[0]:cols[1]])

        hf_ref[0, sl, :] = proj(_C_F).astype(hf_ref.dtype)
        rotary(proj(_C_Q), q_ref, sl, 1.0)
        rotary(proj(_C_K), k_ref, sl, RET_DK ** -0.5)
        v_ref[0, sl, :] = proj(_C_V).astype(v_ref.dtype)
        hg = proj(_C_G)
        sg_ref[0, sl, :] = (hg * _sigmoid(hg)).astype(sg_ref.dtype)
        gt_ref[0, sl, :] = _sigmoid(proj(_C_GATES)).astype(gt_ref.dtype)


def _mix_in(x3, gain, w, cosf, sinf, *, hf_dtype, tm=1024, rows=256):
    b, s, d = x3.shape
    n = w.shape[1]

    def tok(width):
        return pl.BlockSpec((1, tm, width), lambda i, j: (i, j, 0))

    def out(width, dtype=BF16):
        return jax.ShapeDtypeStruct((b, s, width), dtype)

    return pl.pallas_call(
        functools.partial(_mix_in_kernel, rows=rows),
        grid=(b, s // tm),
        in_specs=[
            tok(d), _resident((1, d)), _resident((d, n)),
            pl.BlockSpec((tm, RET_DK), lambda i, j: (j, 0)),
            pl.BlockSpec((tm, RET_DK), lambda i, j: (j, 0)),
        ],
        out_specs=[tok(FOURIER_WIDTH), tok(RET_QK_WIDTH), tok(RET_QK_WIDTH),
                   tok(RET_V_WIDTH), tok(RET_V_WIDTH), tok(2 * D_MODEL)],
        out_shape=[out(FOURIER_WIDTH, hf_dtype), out(RET_QK_WIDTH), out(RET_QK_WIDTH),
                   out(RET_V_WIDTH), out(RET_V_WIDTH, F32), out(2 * D_MODEL)],
        compiler_params=_params(2),
        name="mix_in",
    )(x3, gain, w, cosf, sinf)


def _dft_direct_kernel(a_ref, f_ref, xr_ref, xi_ref):
    s = a_ref.shape[1]
    res = _dot(f_ref[...], a_ref[0])
    xr_ref[0] = res[:s].astype(BF16)
    xi_ref[0] = res[s:].astype(BF16)


def _dft_direct(hf, fmat, *, bw=256):
    b, s, c = hf.shape
    spec = pl.BlockSpec((1, s, bw), lambda i, j: (i, 0, j))
    return pl.pallas_call(
        _dft_direct_kernel,
        grid=(b, c // bw),
        in_specs=[spec, _resident((2 * s, s))],
        out_specs=[spec, spec],
        out_shape=[jax.ShapeDtypeStruct((b, s, c), BF16)] * 2,
        compiler_params=_params(2),
        name="dft_direct",
    )(hf, fmat)


def _lane_blocks(c):
    return [slice(q * _V7X_LANES, (q + 1) * _V7X_LANES) for q in range(c // _V7X_LANES)]


def _dft_stage1_kernel(a_ref, f_ref, tc_ref, ts_ref, tr_ref, ti_ref, a2, tr2, ti2):
    n1 = a_ref.shape[1]
    slab = (n1, _V7X_SUBLANES, _V7X_LANES)
    for q, ls in enumerate(_lane_blocks(a_ref.shape[-1])):
        a2[q] = a_ref[0, :, :, ls].reshape(a2.shape[1:])
        for j in range(_V7X_SUBLANES):
            pick = pl.ds(j, n1, stride=_V7X_SUBLANES)
            res = _dot(f_ref[...], a2[q, pick, :].astype(BF16))
            ur, ui = res[:n1], res[n1:]
            cw, sw = tc_ref[j], ts_ref[j]
            tr2[q, pick, :] = ur * cw + ui * sw
            ti2[q, pick, :] = ui * cw - ur * sw
        tr_ref[0, :, :, ls] = tr2[q].reshape(slab)
        ti_ref[0, :, :, ls] = ti2[q].reshape(slab)


def _dft_stage1(hf, f1, twc, tws, *, n1):
    b, s, c = hf.shape
    n2 = s // n1
    blk = pl.BlockSpec((1, n1, _V7X_SUBLANES, c), lambda i, j: (i, 0, j, 0))
    tw = pl.BlockSpec((_V7X_SUBLANES, n1, _V7X_LANES), lambda i, j: (j, 0, 0))
    rows = pltpu.VMEM((c // _V7X_LANES, n1 * _V7X_SUBLANES, _V7X_LANES), F32)
    return pl.pallas_call(
        _dft_stage1_kernel,
        grid=(b, n2 // _V7X_SUBLANES),
        in_specs=[blk, _resident((2 * n1, n1)), tw, tw],
        out_specs=[blk, blk],
        out_shape=[jax.ShapeDtypeStruct((b, n1, n2, c), F32)] * 2,
        scratch_shapes=[rows, rows, rows],
        compiler_params=_params(2),
        name="dft_stage1",
    )(hf.reshape(b, n1, n2, c), f1, twc, tws)


def _dft_stage2_kernel(tr_ref, ti_ref, f_ref, xr_ref, xi_ref, xr2, xi2):
    n2 = tr_ref.shape[2]
    slab = (n2, _V7X_SUBLANES, _V7X_LANES)
    for q, ls in enumerate(_lane_blocks(tr_ref.shape[-1])):
        for j in range(_V7X_SUBLANES):
            t = jnp.concatenate([tr_ref[0, j, :, ls], ti_ref[0, j, :, ls]], axis=0).astype(BF16)
            res = _dot(f_ref[...], t)
            pick = pl.ds(j, n2, stride=_V7X_SUBLANES)
            xr2[q, pick, :] = res[:n2]
            xi2[q, pick, :] = res[n2:]
        xr_ref[0, :, :, ls] = xr2[q].reshape(slab)
        xi_ref[0, :, :, ls] = xi2[q].reshape(slab)


def _dft_stage2(tr, ti, f2):
    b, n1, n2, c = tr.shape
    inb = pl.BlockSpec((1, _V7X_SUBLANES, n2, c), lambda i, j: (i, j, 0, 0))
    outb = pl.BlockSpec((1, n2, _V7X_SUBLANES, c), lambda i, j: (i, 0, j, 0))
    rows = pltpu.VMEM((c // _V7X_LANES, n2 * _V7X_SUBLANES, _V7X_LANES), F32)
    xr, xi = pl.pallas_call(
        _dft_stage2_kernel,
        grid=(b, n1 // _V7X_SUBLANES),
        in_specs=[inb, inb, _resident((2 * n2, 2 * n2))],
        out_specs=[outb, outb],
        out_shape=[jax.ShapeDtypeStruct((b, n2, n1, c), F32)] * 2,
        scratch_shapes=[rows, rows],
        compiler_params=_params(2),
        name="dft_stage2",
    )(tr, ti, f2)
    return xr.reshape(b, n2 * n1, c), xi.reshape(b, n2 * n1, c)


def _ret_bwd_kernel(dec_ref, k_ref, v_ref, rb_ref, zb_ref, gc_ref, st_ref, *, cc, nch):
    b = pl.program_id(0)
    t = pl.program_id(1)

    @pl.when((b == 0) & (t == 0))
    def _():
        rowi = lax.broadcasted_iota(jnp.int32, (cc, RET_DV), 0).astype(F32)
        for h in range(RET_HEADS):
            lgb = _log_sigmoid(dec_ref[1, h])
            zb_ref[h] = jnp.exp(lgb * rowi)[:, :RET_DK]
            gc_ref[h] = jnp.exp(lgb * float(cc))

    @pl.when(t == 0)
    def _():
        st_ref[...] = jnp.zeros_like(st_ref)

    for h in range(RET_HEADS):
        st = st_ref[h]
        for c in reversed(range(nch)):
            rs = slice(c * cc, (c + 1) * cc)
            rb_ref[0, c, h * RET_DK:(h + 1) * RET_DK, :] = st.astype(BF16)
            kh = k_ref[0, rs, h * RET_DK:(h + 1) * RET_DK]
            vh = v_ref[0, rs, h * RET_DV:(h + 1) * RET_DV]
            kz = (kh.astype(F32) * zb_ref[h]).astype(BF16)
            st = gc_ref[h] * st + _dot_tn(kz, vh)
        st_ref[h] = st


def _ret_bwd(dec, k, v, *, cc, nch=4):
    b, s, _ = k.shape
    ns = s // (cc * nch)
    return pl.pallas_call(
        functools.partial(_ret_bwd_kernel, cc=cc, nch=nch),
        grid=(b, ns),
        in_specs=[
            _resident(dec.shape),
            pl.BlockSpec((1, cc * nch, RET_QK_WIDTH), lambda i, t: (i, ns - 1 - t, 0)),
            pl.BlockSpec((1, cc * nch, RET_V_WIDTH), lambda i, t: (i, ns - 1 - t, 0)),
        ],
        out_specs=pl.BlockSpec((1, nch, RET_QK_WIDTH, RET_DV), lambda i, t: (i, ns - 1 - t, 0, 0)),
        out_shape=jax.ShapeDtypeStruct((b, s // cc, RET_QK_WIDTH, RET_DV), BF16),
        scratch_shapes=[
            pltpu.VMEM((RET_HEADS, cc, RET_DK), F32),
            pltpu.VMEM((RET_HEADS, 1, RET_DV), F32),
            pltpu.VMEM((RET_HEADS, RET_DK, RET_DV), F32),
        ],
        compiler_params=_params(2),
        name="ret_bwd",
    )(dec, k, v)


def _ret_fwd_kernel(dec_ref, q_ref, k_ref, v_ref, sg_ref, rb_ref, o_ref,
                    dm_ref, xif_ref, xib_ref, zf_ref, gc_ref, st_ref, *, cc, nch):
    b = pl.program_id(0)
    j = pl.program_id(1)

    @pl.when((b == 0) & (j == 0))
    def _():
        row = lax.broadcasted_iota(jnp.int32, (cc, cc), 0)
        col = lax.broadcasted_iota(jnp.int32, (cc, cc), 1)
        diff = (row - col).astype(F32)
        rowi = lax.broadcasted_iota(jnp.int32, (cc, RET_DV), 0).astype(F32)
        for h in range(RET_HEADS):
            lgf = _log_sigmoid(dec_ref[0, h])
            lgb = _log_sigmoid(dec_ref[1, h])
            dm_ref[h] = jnp.where(diff >= 0.0,
                                  jnp.exp(lgf * jnp.maximum(diff, 0.0)),
                                  jnp.exp(lgb * jnp.maximum(-diff, 0.0)))
            xif_ref[h] = jnp.exp(lgf * (rowi + 1.0))[:, :RET_DK]
            xib_ref[h] = jnp.exp(lgb * (float(cc) - rowi))[:, :RET_DK]
            zf_ref[h] = jnp.exp(lgf * (float(cc) - 1.0 - rowi))[:, :RET_DK]
            gc_ref[h] = jnp.exp(lgf * float(cc))

    @pl.when(j == 0)
    def _():
        st_ref[...] = jnp.zeros_like(st_ref)

    for h in range(RET_HEADS):
        st = st_ref[h]
        for c in range(nch):
            rs = slice(c * cc, (c + 1) * cc)
            qh = q_ref[0, rs, h * RET_DK:(h + 1) * RET_DK]
            kh = k_ref[0, rs, h * RET_DK:(h + 1) * RET_DK]
            vh = v_ref[0, rs, h * RET_DV:(h + 1) * RET_DV]
            p = (_dot_nt(qh, kh) * dm_ref[h]).astype(BF16)
            qf32 = qh.astype(F32)
            qf = (qf32 * xif_ref[h]).astype(BF16)
            qb = (qf32 * xib_ref[h]).astype(BF16)
            lhs = jnp.concatenate([p, qf, qb], axis=1)
            rhs = jnp.concatenate(
                [vh, st.astype(BF16), rb_ref[0, c, h * RET_DK:(h + 1) * RET_DK, :]], axis=0)
            y = _dot(lhs, rhs)
            kz = (kh.astype(F32) * zf_ref[h]).astype(BF16)
            st = gc_ref[h] * st + _dot_tn(kz, vh)
            ms = jnp.mean(y * y, axis=-1, keepdims=True)
            yn = y * lax.rsqrt(ms + EPS)
            sg = sg_ref[0, rs, h * RET_DV:(h + 1) * RET_DV]
            o_ref[0, rs, h * RET_DV:(h + 1) * RET_DV] = (sg * yn).astype(BF16)
        st_ref[h] = st


def _ret_fwd(dec, q, k, v, sg, rb, *, cc, nch=4):
    b, s, _ = q.shape
    ns = s // (cc * nch)

    def tok(width):
        return pl.BlockSpec((1, cc * nch, width), lambda i, j: (i, j, 0))

    return pl.pallas_call(
        functools.partial(_ret_fwd_kernel, cc=cc, nch=nch),
        grid=(b, ns),
        in_specs=[
            _resident(dec.shape),
            tok(RET_QK_WIDTH), tok(RET_QK_WIDTH), tok(RET_V_WIDTH), tok(RET_V_WIDTH),
            pl.BlockSpec((1, nch, RET_QK_WIDTH, RET_DV), lambda i, j: (i, j, 0, 0)),
        ],
        out_specs=tok(RET_V_WIDTH),
        out_shape=jax.ShapeDtypeStruct((b, s, RET_V_WIDTH), BF16),
        scratch_shapes=[
            pltpu.VMEM((RET_HEADS, cc, cc), F32),
            pltpu.VMEM((RET_HEADS, cc, RET_DK), F32),
            pltpu.VMEM((RET_HEADS, cc, RET_DK), F32),
            pltpu.VMEM((RET_HEADS, cc, RET_DK), F32),
            pltpu.VMEM((RET_HEADS, 1, RET_DV), F32),
            pltpu.VMEM((RET_HEADS, RET_DK, RET_DV), F32),
        ],
        compiler_params=_params(2),
        name="ret_fwd",
    )(dec, q, k, v, sg, rb)


def _mix_out_kernel(x_ref, xr_ref, xi_ref, yr_ref, gt_ref, bdr_ref, bdi_ref, fw_ref, rw_ref, mw_ref,
                    o_ref, *, scale, rows):
    d = x_ref.shape[-1]
    for sl in _row_slices(x_ref.shape[0], rows):
        yf = (_dot(xr_ref[sl, :].astype(BF16), bdr_ref[...])
              + _dot(xi_ref[sl, :].astype(BF16), bdi_ref[...])) * scale
        ya = _dot(yf.astype(BF16), fw_ref[...])
        yb = _dot(yr_ref[sl, :], rw_ref[...])
        ga = gt_ref[sl, :d].astype(F32)
        gb = gt_ref[sl, d:].astype(F32)
        merged = (ga * ya + gb * yb).astype(BF16)
        o_ref[sl, :] = x_ref[sl, :] + _dot(merged, mw_ref[...])


def _mix_out(x2, xr, xi, yr, gt, bdr, bdi, fw, rw, mw, *, scale, tm=1024, rows=512):
    t, d = x2.shape

    def tok(width):
        return pl.BlockSpec((tm, width), lambda i: (i, 0))

    return pl.pallas_call(
        functools.partial(_mix_out_kernel, scale=scale, rows=rows),
        grid=(t // tm,),
        in_specs=[tok(d), tok(FOURIER_WIDTH), tok(FOURIER_WIDTH), tok(RET_V_WIDTH), tok(2 * d),
                  _resident(bdr.shape), _resident(bdi.shape), _resident(fw.shape),
                  _resident(rw.shape), _resident(mw.shape)],
        out_specs=tok(d),
        out_shape=jax.ShapeDtypeStruct((t, d), F32),
        compiler_params=_params(1),
        name="mix_out",
    )(x2, xr, xi, yr, gt, bdr, bdi, fw, rw, mw)


def _mem_kv_kernel(m_ref, g_ref, w_ref, o_ref):
    mn = _rmsnorm(m_ref[...], g_ref[...]).astype(BF16)
    o_ref[...] = _dot(mn, w_ref[...]).astype(BF16)


def _mem_kv(mem2, gain, w, *, tm=256):
    t, d = mem2.shape
    n = w.shape[1]
    return pl.pallas_call(
        _mem_kv_kernel,
        grid=(t // tm,),
        in_specs=[pl.BlockSpec((tm, d), lambda i: (i, 0)), _resident((1, d)), _resident((d, n))],
        out_specs=pl.BlockSpec((tm, n), lambda i: (i, 0)),
        out_shape=jax.ShapeDtypeStruct((t, n), BF16),
        compiler_params=_params(1),
        name="mem_kv",
    )(mem2, gain, w)


def _xattn_kernel(x_ref, g_ref, wq_ref, kv_ref, wo_ref, o_ref, *, rows):
    d = x_ref.shape[-1]
    heads = [(h * XA_HEAD_DIM, (h + 1) * XA_HEAD_DIM) for h in range(XA_HEADS)]
    sls = _row_slices(x_ref.shape[1], rows)
    xs = [x_ref[0, sl, :] for sl in sls]
    qs = [_dot(_rmsnorm(x, g_ref[...]).astype(BF16), wq_ref[...]).astype(BF16) for x in xs]
    scores = [[_dot_nt(q[:, lo:hi], kv_ref[0, :, lo:hi]) * (XA_HEAD_DIM ** -0.5) for lo, hi in heads]
              for q in qs]
    probs = []
    for per_head in scores:
        row = []
        for s in per_head:
            e = jnp.exp(s - jnp.max(s, axis=-1, keepdims=True))
            row.append((e / jnp.sum(e, axis=-1, keepdims=True)).astype(BF16))
        probs.append(row)
    outs = [[_dot(p, kv_ref[0, :, d + lo:d + hi]).astype(BF16) for p, (lo, hi) in zip(row, heads)]
            for row in probs]
    for sl, x, per_head in zip(sls, xs, outs):
        o_ref[0, sl, :] = x + _dot(jnp.concatenate(per_head, axis=-1), wo_ref[...])


def _xattn(x3, gain, wq, kv, wo, *, tm=1024, rows=512):
    b, s, d = x3.shape
    m = kv.shape[1]
    tok = pl.BlockSpec((1, tm, d), lambda i, j: (i, j, 0))
    return pl.pallas_call(
        functools.partial(_xattn_kernel, rows=rows),
        grid=(b, s // tm),
        in_specs=[tok, _resident((1, d)), _resident((d, d)),
                  pl.BlockSpec((1, m, 2 * d), lambda i, j: (i, 0, 0)),
                  _resident((d, d))],
        out_specs=tok,
        out_shape=jax.ShapeDtypeStruct((b, s, d), F32),
        compiler_params=_params(2),
        name="xattn",
    )(x3, gain, wq, kv, wo)


def _angle(num, n):
    return (2.0 * math.pi / n) * (num % n).astype(F32)


def _dft_cos_sin(n):
    i = jnp.arange(n, dtype=jnp.int32)
    ang = _angle(i[:, None] * i[None, :], n)
    return jnp.cos(ang), jnp.sin(ang)


def _rotary_tables(s):
    half = RET_DK // 2
    inv = 1.0 / (ROPE_BASE ** (jnp.arange(half, dtype=F32) * 2.0 / RET_DK))
    ang = jnp.arange(s, dtype=F32)[:, None] * inv[None, :]
    cos, sin = jnp.cos(ang), jnp.sin(ang)
    return jnp.concatenate([cos, cos], -1), jnp.concatenate([-sin, sin], -1)


def _seq_tables(s):
    tabs = {"rot": _rotary_tables(s)}
    if s >= _DFT_TWO_STAGE_MIN:
        n1, n2 = _DFT_N1, s // _DFT_N1
        c1, s1 = _dft_cos_sin(n1)
        tabs["f1"] = jnp.concatenate([c1, -s1], 0).astype(BF16)
        c2, s2 = _dft_cos_sin(n2)
        tabs["f2"] = jnp.concatenate(
            [jnp.concatenate([c2, s2], 1), jnp.concatenate([-s2, c2], 1)], 0).astype(BF16)
        ang = _angle(jnp.arange(n2, dtype=jnp.int32)[:, None] * jnp.arange(n1, dtype=jnp.int32)[None, :], s)
        lane = (n2, n1, _V7X_LANES)
        tabs["tw"] = (jnp.broadcast_to(jnp.cos(ang)[:, :, None], lane),
                      jnp.broadcast_to(jnp.sin(ang)[:, :, None], lane))
    else:
        c, sn = _dft_cos_sin(s)
        tabs["f"] = jnp.concatenate([c, -sn], 0).astype(BF16)
    return tabs


def _channel_dft_blocks():
    c, s = _dft_cos_sin(FOURIER_GROUP_DIM)
    eye = jnp.eye(FOURIER_WIDTH // FOURIER_GROUP_DIM, dtype=F32)
    return jnp.kron(eye, c).astype(BF16), jnp.kron(eye, s).astype(BF16)


def _trunk(x, mem, w, bd):
    b, s, d = x.shape
    m = mem.shape[1]
    tabs = _seq_tables(s)
    cosf, sinf = tabs["rot"]
    two_stage = "f" not in tabs
    scale = float((s * FOURIER_GROUP_DIM) ** -0.5)
    x2 = x.reshape(b * s, d)
    mem2 = mem.reshape(b * m, d)
    for l in range(DEPTH):
        lw = w[l]
        x2 = _ffn(x2, lw["ffn1_norm"], lw["ffn1_wg"], lw["ffn1_wu"], lw["ffn1_wo"], w["final_norm"],
                  final_norm=False)
        hf, q, k, v, sg, gt = _mix_in(x2.reshape(b, s, d), lw["mix_norm"], lw["mix_w_in"], cosf, sinf,
                                      hf_dtype=F32 if two_stage else BF16)
        if two_stage:
            tr, ti = _dft_stage1(hf, tabs["f1"], *tabs["tw"], n1=_DFT_N1)
            xr, xi = _dft_stage2(tr, ti, tabs["f2"])
        else:
            xr, xi = _dft_direct(hf, tabs["f"])
        rb = _ret_bwd(lw["dec"], k, v, cc=_RET_CHUNK)
        yr = _ret_fwd(lw["dec"], q, k, v, sg, rb, cc=_RET_CHUNK)
        x2 = _mix_out(x2, xr.reshape(b * s, -1), xi.reshape(b * s, -1), yr.reshape(b * s, -1),
                      gt.reshape(b * s, -1), bd[0], bd[1], lw["fourier_w"], lw["ret_w_out"],
                      lw["mix_w_out"], scale=scale)
        kv = _mem_kv(mem2, lw["mem_norm"], lw["xa_wkv"]).reshape(b, m, 2 * d)
        x2 = _xattn(x2.reshape(b, s, d), lw["xa_norm"], lw["xa_wq"], kv, lw["xa_wo"]).reshape(b * s, d)
        x2 = _ffn(x2, lw["ffn2_norm"], lw["ffn2_wg"], lw["ffn2_wu"], lw["ffn2_wo"], w["final_norm"],
                  final_norm=(l == DEPTH - 1))
    return x2.reshape(b, s, d)


def kernel(x_prompt, x_sample, mem_prompt, mem_sample, ffn1_norm, ffn1_w_in, ffn1_w_out, mix_norm, mix_w_in, fourier_w, ret_decay_fwd, ret_decay_bwd, ret_w_out, mix_w_out, xa_norm, mem_norm, xa_wq, xa_wkv, xa_wo, ffn2_norm, ffn2_w_in, ffn2_w_out, final_norm):
    bf = lambda a: a.astype(BF16)
    row = lambda a: a.reshape(1, -1)
    w = {"final_norm": row(final_norm)}
    for l in range(DEPTH):
        dec = jnp.stack([ret_decay_fwd[l], ret_decay_bwd[l]])
        w[l] = {
            "ffn1_norm": row(ffn1_norm[l]),
            "ffn1_wg": bf(ffn1_w_in[l][:, :D_FF]), "ffn1_wu": bf(ffn1_w_in[l][:, D_FF:]),
            "ffn1_wo": bf(ffn1_w_out[l]),
            "mix_norm": row(mix_norm[l]), "mix_w_in": bf(mix_w_in[l]),
            "fourier_w": bf(fourier_w[l]), "ret_w_out": bf(ret_w_out[l]), "mix_w_out": bf(mix_w_out[l]),
            "dec": jnp.broadcast_to(dec[:, :, None, None], (2, RET_HEADS, 1, RET_DV)),
            "xa_norm": row(xa_norm[l]), "mem_norm": row(mem_norm[l]),
            "xa_wq": bf(xa_wq[l]), "xa_wkv": bf(xa_wkv[l]), "xa_wo": bf(xa_wo[l]),
            "ffn2_norm": row(ffn2_norm[l]),
            "ffn2_wg": bf(ffn2_w_in[l][:, :D_FF]), "ffn2_wu": bf(ffn2_w_in[l][:, D_FF:]),
            "ffn2_wo": bf(ffn2_w_out[l]),
        }
    bd = _channel_dft_blocks()
    y_prompt = _trunk(x_prompt, mem_prompt, w, bd)
    y_sample = _trunk(x_sample, mem_sample, w, bd)
    return (y_prompt, y_sample)
```

```python
import functools
import math

import jax
import jax.numpy as jnp
from jax import lax
from jax.experimental import pallas as pl
from jax.experimental.pallas import tpu as pltpu

F32 = jnp.float32
BF16 = jnp.bfloat16

D_MODEL = 1024
DEPTH = 4
D_FF = 2816
FOURIER_GROUP_DIM = 128
FOURIER_WIDTH = 512
RET_HEADS = 4
RET_DK = 128
RET_DV = 256
RET_QK_WIDTH = RET_HEADS * RET_DK
RET_V_WIDTH = RET_HEADS * RET_DV
XA_HEADS = 4
XA_HEAD_DIM = D_MODEL // XA_HEADS
ROPE_BASE = 10000.0
EPS = 1e-6

_C_F = (0, FOURIER_WIDTH)
_C_Q = (_C_F[1], _C_F[1] + RET_QK_WIDTH)
_C_K = (_C_Q[1], _C_Q[1] + RET_QK_WIDTH)
_C_V = (_C_K[1], _C_K[1] + RET_V_WIDTH)
_C_G = (_C_V[1], _C_V[1] + RET_V_WIDTH)
_C_GATES = (_C_G[1], _C_G[1] + 2 * D_MODEL)

_V7X_VMEM_BYTES = 64 * 1024 * 1024
_VMEM_LIMIT = _V7X_VMEM_BYTES - 8 * 1024 * 1024
_V7X_LANES = 128
_V7X_SUBLANES = 8

_RET_CHUNK = 256
_DFT_TWO_STAGE_MIN = 16384
_DFT_N1 = 128


def _params(n_axes):
    return pltpu.CompilerParams(
        dimension_semantics=("arbitrary",) * n_axes, vmem_limit_bytes=_VMEM_LIMIT)


def _resident(shape):
    nd = len(shape)
    return pl.BlockSpec(shape, lambda *_: (0,) * nd, pipeline_mode=pl.Buffered(1))


def _rmsnorm(x, g):
    ms = jnp.mean(x * x, axis=-1, keepdims=True)
    return x * lax.rsqrt(ms + EPS) * g


def _sigmoid(x):
    return 1.0 / (1.0 + jnp.exp(-x))


def _log_sigmoid(x):
    return -(jnp.maximum(-x, 0.0) + jnp.log1p(jnp.exp(-jnp.abs(x))))


def _dot(a, b):
    return jnp.dot(a, b, preferred_element_type=F32)


def _dot_nt(a, b):
    return lax.dot_general(a, b, (((1,), (1,)), ((), ())), preferred_element_type=F32)


def _dot_tn(a, b):
    return lax.dot_general(a, b, (((0,), (0,)), ((), ())), preferred_element_type=F32)


def _row_slices(total, rows):
    return [slice(r * rows, (r + 1) * rows) for r in range(total // rows)]


def _ffn_kernel(x_ref, g_ref, wg_ref, wu_ref, wo_ref, fg_ref, o_ref, *, rows, tf, final_norm):
    f = wg_ref.shape[1]
    for sl in _row_slices(x_ref.shape[0], rows):
        x = x_ref[sl, :]
        xn = _rmsnorm(x, g_ref[...]).astype(BF16)
        acc = None
        for c in range(f // tf):
            cs = slice(c * tf, (c + 1) * tf)
            gate = _dot(xn, wg_ref[:, cs])
            up = _dot(xn, wu_ref[:, cs])
            act = (gate * _sigmoid(gate) * up).astype(BF16)
            part = _dot(act, wo_ref[cs, :])
            acc = part if acc is None else acc + part
        y = x + 0.5 * acc
        if final_norm:
            y = _rmsnorm(y, fg_ref[...])
        o_ref[sl, :] = y


def _ffn(x2, gain, wg, wu, wo, final_gain, *, final_norm, tm=1024, rows=256, tf=1408):
    t, d = x2.shape
    f = wg.shape[1]
    return pl.pallas_call(
        functools.partial(_ffn_kernel, rows=rows, tf=tf, final_norm=final_norm),
        grid=(t // tm,),
        in_specs=[
            pl.BlockSpec((tm, d), lambda i: (i, 0)),
            _resident((1, d)), _resident((d, f)), _resident((d, f)), _resident((f, d)), _resident((1, d)),
        ],
        out_specs=pl.BlockSpec((tm, d), lambda i: (i, 0)),
        out_shape=jax.ShapeDtypeStruct((t, d), F32),
        compiler_params=_params(1),
        name="ffn",
    )(x2, gain, wg, wu, wo, final_gain)


def _mix_in_kernel(x_ref, g_ref, w_ref, cos_ref, sin_ref,
                   hf_ref, q_ref, k_ref, v_ref, sg_ref, gt_ref, *, rows):
    def rotary(hx, out_ref, sl, scale):
        cosf = cos_ref[sl, :]
        sinf = sin_ref[sl, :]
        for h in range(RET_HEADS):
            blk = hx[:, h * RET_DK:(h + 1) * RET_DK]
            rot = blk * cosf + pltpu.roll(blk, RET_DK // 2, 1) * sinf
            if scale != 1.0:
                rot = rot * scale
            out_ref[0, sl, h * RET_DK:(h + 1) * RET_DK] = rot.astype(out_ref.dtype)

    for sl in _row_slices(x_ref.shape[1], rows):
        xn = _rmsnorm(x_ref[0, sl, :], g_ref[...]).astype(BF16)

        def proj(cols):
            return _dot(xn, w_ref[:, cols[0]:cols[1]])

        hf_ref[0, sl, :] = proj(_C_F).astype(hf_ref.dtype)
        rotary(proj(_C_Q), q_ref, sl, 1.0)
        rotary(proj(_C_K), k_ref, sl, RET_DK ** -0.5)
        v_ref[0, sl, :] = proj(_C_V).astype(v_ref.dtype)
        hg = proj(_C_G)
        sg_ref[0, sl, :] = (hg * _sigmoid(hg)).astype(sg_ref.dtype)
        gt_ref[0, sl, :] = _sigmoid(proj(_C_GATES)).astype(gt_ref.dtype)


def _mix_in(x3, gain, w, cosf, sinf, *, hf_dtype, tm=1024, rows=256):
    b, s, d = x3.shape
    n = w.shape[1]

    def tok(width):
        return pl.BlockSpec((1, tm, width), lambda i, j: (i, j, 0))

    def out(width, dtype=BF16):
        return jax.ShapeDtypeStruct((b, s, width), dtype)

    return pl.pallas_call(
        functools.partial(_mix_in_kernel, rows=rows),
        grid=(b, s // tm),
        in_specs=[
            tok(d), _resident((1, d)), _resident((d, n)),
            pl.BlockSpec((tm, RET_DK), lambda i, j: (j, 0)),
            pl.BlockSpec((tm, RET_DK), lambda i, j: (j, 0)),
        ],
        out_specs=[tok(FOURIER_WIDTH), tok(RET_QK_WIDTH), tok(RET_QK_WIDTH),
                   tok(RET_V_WIDTH), tok(RET_V_WIDTH), tok(2 * D_MODEL)],
        out_shape=[out(FOURIER_WIDTH, hf_dtype), out(RET_QK_WIDTH), out(RET_QK_WIDTH),
                   out(RET_V_WIDTH), out(RET_V_WIDTH), out(2 * D_MODEL)],
        compiler_params=_params(2),
        name="mix_in",
    )(x3, gain, w, cosf, sinf)


def _dft_direct_kernel(a_ref, f_ref, xr_ref, xi_ref):
    s = a_ref.shape[1]
    res = _dot(f_ref[...], a_ref[0])
    xr_ref[0] = res[:s].astype(BF16)
    xi_ref[0] = res[s:].astype(BF16)


def _dft_direct(hf, fmat, *, bw=256):
    b, s, c = hf.shape
    spec = pl.BlockSpec((1, s, bw), lambda i, j: (i, 0, j))
    return pl.pallas_call(
        _dft_direct_kernel,
        grid=(b, c // bw),
        in_specs=[spec, _resident((2 * s, s))],
        out_specs=[spec, spec],
        out_shape=[jax.ShapeDtypeStruct((b, s, c), BF16)] * 2,
        compiler_params=_params(2),
        name="dft_direct",
    )(hf, fmat)


def _lane_blocks(c):
    return [slice(q * _V7X_LANES, (q + 1) * _V7X_LANES) for q in range(c // _V7X_LANES)]


def _dft_stage1_kernel(a_ref, f_ref, tc_ref, ts_ref, tr_ref, ti_ref, a2, tr2, ti2):
    n1 = a_ref.shape[1]
    slab = (n1, _V7X_SUBLANES, _V7X_LANES)
    for q, ls in enumerate(_lane_blocks(a_ref.shape[-1])):
        a2[q] = a_ref[0, :, :, ls].reshape(a2.shape[1:])
        for j in range(_V7X_SUBLANES):
            pick = pl.ds(j, n1, stride=_V7X_SUBLANES)
            res = _dot(f_ref[...], a2[q, pick, :].astype(BF16))
            ur, ui = res[:n1], res[n1:]
            cw, sw = tc_ref[j], ts_ref[j]
            tr2[q, pick, :] = ur * cw + ui * sw
            ti2[q, pick, :] = ui * cw - ur * sw
        tr_ref[0, :, :, ls] = tr2[q].reshape(slab)
        ti_ref[0, :, :, ls] = ti2[q].reshape(slab)


def _dft_stage1(hf, f1, twc, tws, *, n1):
    b, s, c = hf.shape
    n2 = s // n1
    blk = pl.BlockSpec((1, n1, _V7X_SUBLANES, c), lambda i, j: (i, 0, j, 0))
    tw = pl.BlockSpec((_V7X_SUBLANES, n1, _V7X_LANES), lambda i, j: (j, 0, 0))
    rows = pltpu.VMEM((c // _V7X_LANES, n1 * _V7X_SUBLANES, _V7X_LANES), F32)
    return pl.pallas_call(
        _dft_stage1_kernel,
        grid=(b, n2 // _V7X_SUBLANES),
        in_specs=[blk, _resident((2 * n1, n1)), tw, tw],
        out_specs=[blk, blk],
        out_shape=[jax.ShapeDtypeStruct((b, n1, n2, c), F32)] * 2,
        scratch_shapes=[rows, rows, rows],
        compiler_params=_params(2),
        name="dft_stage1",
    )(hf.reshape(b, n1, n2, c), f1, twc, tws)


def _dft_stage2_kernel(tr_ref, ti_ref, f_ref, xr_ref, xi_ref, xr2, xi2):
    n2 = tr_ref.shape[2]
    slab = (n2, _V7X_SUBLANES, _V7X_LANES)
    for q, ls in enumerate(_lane_blocks(tr_ref.shape[-1])):
        for j in range(_V7X_SUBLANES):
            t = jnp.concatenate([tr_ref[0, j, :, ls], ti_ref[0, j, :, ls]], axis=0).astype(BF16)
            res = _dot(f_ref[...], t)
            pick = pl.ds(j, n2, stride=_V7X_SUBLANES)
            xr2[q, pick, :] = res[:n2]
            xi2[q, pick, :] = res[n2:]
        xr_ref[0, :, :, ls] = xr2[q].reshape(slab)
        xi_ref[0, :, :, ls] = xi2[q].reshape(slab)


def _dft_stage2(tr, ti, f2):
    b, n1, n2, c = tr.shape
    inb = pl.BlockSpec((1, _V7X_SUBLANES, n2, c), lambda i, j: (i, j, 0, 0))
    outb = pl.BlockSpec((1, n2, _V7X_SUBLANES, c), lambda i, j: (i, 0, j, 0))
    rows = pltpu.VMEM((c // _V7X_LANES, n2 * _V7X_SUBLANES, _V7X_LANES), F32)
    xr, xi = pl.pallas_call(
        _dft_stage2_kernel,
        grid=(b, n1 // _V7X_SUBLANES),
        in_specs=[inb, inb, _resident((2 * n2, 2 * n2))],
        out_specs=[outb, outb],
        out_shape=[jax.ShapeDtypeStruct((b, n2, n1, c), F32)] * 2,
        scratch_shapes=[rows, rows],
        compiler_params=_params(2),
        name="dft_stage2",
    )(tr, ti, f2)
    return xr.reshape(b, n2 * n1, c), xi.reshape(b, n2 * n1, c)


def _ret_bwd_kernel(dec_ref, k_ref, v_ref, rb_ref, zb_ref, gc_ref, st_ref, *, cc, nch):
    b = pl.program_id(0)
    t = pl.program_id(1)

    @pl.when((b == 0) & (t == 0))
    def _():
        rowi = lax.broadcasted_iota(jnp.int32, (cc, RET_DV), 0).astype(F32)
        for h in range(RET_HEADS):
            lgb = _log_sigmoid(dec_ref[1, h])
            zb_ref[h] = jnp.exp(lgb * rowi)[:, :RET_DK]
            gc_ref[h] = jnp.exp(lgb * float(cc))

    @pl.when(t == 0)
    def _():
        st_ref[...] = jnp.zeros_like(st_ref)

    for h in range(RET_HEADS):
        st = st_ref[h]
        for c in reversed(range(nch)):
            rs = slice(c * cc, (c + 1) * cc)
            rb_ref[0, c, h * RET_DK:(h + 1) * RET_DK, :] = st.astype(BF16)
            kh = k_ref[0, rs, h * RET_DK:(h + 1) * RET_DK]
            vh = v_ref[0, rs, h * RET_DV:(h + 1) * RET_DV]
            kz = (kh.astype(F32) * zb_ref[h]).astype(BF16)
            st = gc_ref[h] * st + _dot_tn(kz, vh)
        st_ref[h] = st


def _ret_bwd(dec, k, v, *, cc, nch=4):
    b, s, _ = k.shape
    ns = s // (cc * nch)
    return pl.pallas_call(
        functools.partial(_ret_bwd_kernel, cc=cc, nch=nch),
        grid=(b, ns),
        in_specs=[
            _resident(dec.shape),
            pl.BlockSpec((1, cc * nch, RET_QK_WIDTH), lambda i, t: (i, ns - 1 - t, 0)),
            pl.BlockSpec((1, cc * nch, RET_V_WIDTH), lambda i, t: (i, ns - 1 - t, 0)),
        ],
        out_specs=pl.BlockSpec((1, nch, RET_QK_WIDTH, RET_DV), lambda i, t: (i, ns - 1 - t, 0, 0)),
        out_shape=jax.ShapeDtypeStruct((b, s // cc, RET_QK_WIDTH, RET_DV), BF16),
        scratch_shapes=[
            pltpu.VMEM((RET_HEADS, cc, RET_DK), F32),
            pltpu.VMEM((RET_HEADS, 1, RET_DV), F32),
            pltpu.VMEM((RET_HEADS, RET_DK, RET_DV), F32),
        ],
        compiler_params=_params(2),
        name="ret_bwd",
    )(dec, k, v)


def _ret_fwd_kernel(dec_ref, q_ref, k_ref, v_ref, sg_ref, rb_ref, o_ref,
                    dm_ref, xif_ref, xib_ref, zf_ref, gc_ref, st_ref, *, cc, nch):
    b = pl.program_id(0)
    j = pl.program_id(1)

    @pl.when((b == 0) & (j == 0))
    def _():
        row = lax.broadcasted_iota(jnp.int32, (cc, cc), 0)
        col = lax.broadcasted_iota(jnp.int32, (cc, cc), 1)
        diff = (row - col).astype(F32)
        rowi = lax.broadcasted_iota(jnp.int32, (cc, RET_DV), 0).astype(F32)
        for h in range(RET_HEADS):
            lgf = _log_sigmoid(dec_ref[0, h])
            lgb = _log_sigmoid(dec_ref[1, h])
            dm_ref[h] = jnp.where(diff >= 0.0,
                                  jnp.exp(lgf * jnp.maximum(diff, 0.0)),
                                  jnp.exp(lgb * jnp.maximum(-diff, 0.0)))
            xif_ref[h] = jnp.exp(lgf * (rowi + 1.0))[:, :RET_DK]
            xib_ref[h] = jnp.exp(lgb * (float(cc) - rowi))[:, :RET_DK]
            zf_ref[h] = jnp.exp(lgf * (float(cc) - 1.0 - rowi))[:, :RET_DK]
            gc_ref[h] = jnp.exp(lgf * float(cc))

    @pl.when(j == 0)
    def _():
        st_ref[...] = jnp.zeros_like(st_ref)

    for h in range(RET_HEADS):
        st = st_ref[h]
        for c in range(nch):
            rs = slice(c * cc, (c + 1) * cc)
            qh = q_ref[0, rs, h * RET_DK:(h + 1) * RET_DK]
            kh = k_ref[0, rs, h * RET_DK:(h + 1) * RET_DK]
            vh = v_ref[0, rs, h * RET_DV:(h + 1) * RET_DV]
            p = (_dot_nt(qh, kh) * dm_ref[h]).astype(BF16)
            qf32 = qh.astype(F32)
            qf = (qf32 * xif_ref[h]).astype(BF16)
            qb = (qf32 * xib_ref[h]).astype(BF16)
            lhs = jnp.concatenate([p, qf, qb], axis=1)
            rhs = jnp.concatenate(
                [vh, st.astype(BF16), rb_ref[0, c, h * RET_DK:(h + 1) * RET_DK, :]], axis=0)
            y = _dot(lhs, rhs)
            kz = (kh.astype(F32) * zf_ref[h]).astype(BF16)
            st = gc_ref[h] * st + _dot_tn(kz, vh)
            ms = jnp.mean(y * y, axis=-1, keepdims=True)
            yn = y * lax.rsqrt(ms + EPS)
            sg = sg_ref[0, rs, h * RET_DV:(h + 1) * RET_DV].astype(F32)
            o_ref[0, rs, h * RET_DV:(h + 1) * RET_DV] = (sg * yn).astype(BF16)
        st_ref[h] = st


def _ret_fwd(dec, q, k, v, sg, rb, *, cc, nch=4):
    b, s, _ = q.shape
    ns = s // (cc * nch)

    def tok(width):
        return pl.BlockSpec((1, cc * nch, width), lambda i, j: (i, j, 0))

    return pl.pallas_call(
        functools.partial(_ret_fwd_kernel, cc=cc, nch=nch),
        grid=(b, ns),
        in_specs=[
            _resident(dec.shape),
            tok(RET_QK_WIDTH), tok(RET_QK_WIDTH), tok(RET_V_WIDTH), tok(RET_V_WIDTH),
            pl.BlockSpec((1, nch, RET_QK_WIDTH, RET_DV), lambda i, j: (i, j, 0, 0)),
        ],
        out_specs=tok(RET_V_WIDTH),
        out_shape=jax.ShapeDtypeStruct((b, s, RET_V_WIDTH), BF16),
        scratch_shapes=[
            pltpu.VMEM((RET_HEADS, cc, cc), F32),
            pltpu.VMEM((RET_HEADS, cc, RET_DK), F32),
            pltpu.VMEM((RET_HEADS, cc, RET_DK), F32),
            pltpu.VMEM((RET_HEADS, cc, RET_DK), F32),
            pltpu.VMEM((RET_HEADS, 1, RET_DV), F32),
            pltpu.VMEM((RET_HEADS, RET_DK, RET_DV), F32),
        ],
        compiler_params=_params(2),
        name="ret_fwd",
    )(dec, q, k, v, sg, rb)


def _mix_out_kernel(x_ref, xr_ref, xi_ref, yr_ref, gt_ref, cs_ref, fw_ref, rw_ref, mw_ref,
                    o_ref, *, scale, rows):
    d = x_ref.shape[-1]
    for sl in _row_slices(x_ref.shape[0], rows):
        groups = [
            _dot(jnp.concatenate([xr_ref[sl, ls], xi_ref[sl, ls]], axis=1).astype(BF16), cs_ref[...])
            for ls in _lane_blocks(xr_ref.shape[-1])]
        yf = jnp.concatenate(groups, axis=1) * scale
        ya = _dot(yf.astype(BF16), fw_ref[...])
        yb = _dot(yr_ref[sl, :], rw_ref[...])
        ga = gt_ref[sl, :d].astype(F32)
        gb = gt_ref[sl, d:].astype(F32)
        merged = (ga * ya + gb * yb).astype(BF16)
        o_ref[sl, :] = x_ref[sl, :] + _dot(merged, mw_ref[...])


def _mix_out(x2, xr, xi, yr, gt, cs, fw, rw, mw, *, scale, tm=1024, rows=512):
    t, d = x2.shape

    def tok(width):
        return pl.BlockSpec((tm, width), lambda i: (i, 0))

    return pl.pallas_call(
        functools.partial(_mix_out_kernel, scale=scale, rows=rows),
        grid=(t // tm,),
        in_specs=[tok(d), tok(FOURIER_WIDTH), tok(FOURIER_WIDTH), tok(RET_V_WIDTH), tok(2 * d),
                  _resident(cs.shape), _resident(fw.shape), _resident(rw.shape), _resident(mw.shape)],
        out_specs=tok(d),
        out_shape=jax.ShapeDtypeStruct((t, d), F32),
        compiler_params=_params(1),
        name="mix_out",
    )(x2, xr, xi, yr, gt, cs, fw, rw, mw)


def _mem_kv_kernel(m_ref, g_ref, w_ref, o_ref):
    mn = _rmsnorm(m_ref[...], g_ref[...]).astype(BF16)
    o_ref[...] = _dot(mn, w_ref[...]).astype(BF16)


def _mem_kv(mem2, gain, w, *, tm=256):
    t, d = mem2.shape
    n = w.shape[1]
    return pl.pallas_call(
        _mem_kv_kernel,
        grid=(t // tm,),
        in_specs=[pl.BlockSpec((tm, d), lambda i: (i, 0)), _resident((1, d)), _resident((d, n))],
        out_specs=pl.BlockSpec((tm, n), lambda i: (i, 0)),
        out_shape=jax.ShapeDtypeStruct((t, n), BF16),
        compiler_params=_params(1),
        name="mem_kv",
    )(mem2, gain, w)


def _xattn_kernel(x_ref, g_ref, wq_ref, kv_ref, wo_ref, o_ref, *, rows):
    d = x_ref.shape[-1]
    heads = [(h * XA_HEAD_DIM, (h + 1) * XA_HEAD_DIM) for h in range(XA_HEADS)]
    sls = _row_slices(x_ref.shape[1], rows)
    xs = [x_ref[0, sl, :] for sl in sls]
    qs = [_dot(_rmsnorm(x, g_ref[...]).astype(BF16), wq_ref[...]).astype(BF16) for x in xs]
    scores = [[_dot_nt(q[:, lo:hi], kv_ref[0, :, lo:hi]) * (XA_HEAD_DIM ** -0.5) for lo, hi in heads]
              for q in qs]
    probs = []
    for per_head in scores:
        row = []
        for s in per_head:
            e = jnp.exp(s - jnp.max(s, axis=-1, keepdims=True))
            row.append((e / jnp.sum(e, axis=-1, keepdims=True)).astype(BF16))
        probs.append(row)
    outs = [[_dot(p, kv_ref[0, :, d + lo:d + hi]).astype(BF16) for p, (lo, hi) in zip(row, heads)]
            for row in probs]
    for sl, x, per_head in zip(sls, xs, outs):
        o_ref[0, sl, :] = x + _dot(jnp.concatenate(per_head, axis=-1), wo_ref[...])


def _xattn(x3, gain, wq, kv, wo, *, tm=1024, rows=512):
    b, s, d = x3.shape
    m = kv.shape[1]
    tok = pl.BlockSpec((1, tm, d), lambda i, j: (i, j, 0))
    return pl.pallas_call(
        functools.partial(_xattn_kernel, rows=rows),
        grid=(b, s // tm),
        in_specs=[tok, _resident((1, d)), _resident((d, d)),
                  pl.BlockSpec((1, m, 2 * d), lambda i, j: (i, 0, 0)),
                  _resident((d, d))],
        out_specs=tok,
        out_shape=jax.ShapeDtypeStruct((b, s, d), F32),
        compiler_params=_params(2),
        name="xattn",
    )(x3, gain, wq, kv, wo)


def _angle(num, n):
    return (2.0 * math.pi / n) * (num % n).astype(F32)


_DFT_TABLE_SPLIT = 32


def _dft_cos_sin(n):
    i = jnp.arange(n, dtype=jnp.int32)
    if n <= _DFT_TABLE_SPLIT * _V7X_LANES:
        ang = _angle(i[:, None] * i[None, :], n)
        return jnp.cos(ang), jnp.sin(ang)
    na = n // _DFT_TABLE_SPLIT
    ang_a = _angle(i[:, None] * jnp.arange(na, dtype=jnp.int32)[None, :], na)[:, :, None]
    ang_b = _angle(i[:, None] * jnp.arange(_DFT_TABLE_SPLIT, dtype=jnp.int32)[None, :], n)[:, None, :]
    ca, sa, cb, sb = jnp.cos(ang_a), jnp.sin(ang_a), jnp.cos(ang_b), jnp.sin(ang_b)
    return (ca * cb - sa * sb).reshape(n, n), (sa * cb + ca * sb).reshape(n, n)


def _rotary_tables(s):
    half = RET_DK // 2
    inv = 1.0 / (ROPE_BASE ** (jnp.arange(half, dtype=F32) * 2.0 / RET_DK))
    ang = jnp.arange(s, dtype=F32)[:, None] * inv[None, :]
    cos, sin = jnp.cos(ang), jnp.sin(ang)
    return jnp.concatenate([cos, cos], -1), jnp.concatenate([-sin, sin], -1)


def _seq_tables(s):
    tabs = {"rot": _rotary_tables(s)}
    if s >= _DFT_TWO_STAGE_MIN:
        n1, n2 = _DFT_N1, s // _DFT_N1
        c1, s1 = _dft_cos_sin(n1)
        tabs["f1"] = jnp.concatenate([c1, -s1], 0).astype(BF16)
        c2, s2 = _dft_cos_sin(n2)
        tabs["f2"] = jnp.concatenate(
            [jnp.concatenate([c2, s2], 1), jnp.concatenate([-s2, c2], 1)], 0).astype(BF16)
        ang = _angle(jnp.arange(n2, dtype=jnp.int32)[:, None] * jnp.arange(n1, dtype=jnp.int32)[None, :], s)
        lane = (n2, n1, _V7X_LANES)
        tabs["tw"] = (jnp.broadcast_to(jnp.cos(ang)[:, :, None], lane),
                      jnp.broadcast_to(jnp.sin(ang)[:, :, None], lane))
    else:
        c, sn = _dft_cos_sin(s)
        tabs["f"] = jnp.concatenate([c, -sn], 0).astype(BF16)
    return tabs


def _channel_dft_matrix():
    c, s = _dft_cos_sin(FOURIER_GROUP_DIM)
    return jnp.concatenate([c, s], 0).astype(BF16)


def _trunk(x, mem, w, bd):
    b, s, d = x.shape
    m = mem.shape[1]
    tabs = _seq_tables(s)
    cosf, sinf = tabs["rot"]
    two_stage = "f" not in tabs
    scale = float((s * FOURIER_GROUP_DIM) ** -0.5)
    x2 = x.reshape(b * s, d)
    mem2 = mem.reshape(b * m, d)
    for l in range(DEPTH):
        lw = w[l]
        x2 = _ffn(x2, lw["ffn1_norm"], lw["ffn1_wg"], lw["ffn1_wu"], lw["ffn1_wo"], w["final_norm"],
                  final_norm=False)
        hf, q, k, v, sg, gt = _mix_in(x2.reshape(b, s, d), lw["mix_norm"], lw["mix_w_in"], cosf, sinf,
                                      hf_dtype=F32 if two_stage else BF16)
        if two_stage:
            tr, ti = _dft_stage1(hf, tabs["f1"], *tabs["tw"], n1=_DFT_N1)
            xr, xi = _dft_stage2(tr, ti, tabs["f2"])
        else:
            xr, xi = _dft_direct(hf, tabs["f"])
        rb = _ret_bwd(lw["dec"], k, v, cc=_RET_CHUNK)
        yr = _ret_fwd(lw["dec"], q, k, v, sg, rb, cc=_RET_CHUNK)
        x2 = _mix_out(x2, xr.reshape(b * s, -1), xi.reshape(b * s, -1), yr.reshape(b * s, -1),
                      gt.reshape(b * s, -1), bd, lw["fourier_w"], lw["ret_w_out"],
                      lw["mix_w_out"], scale=scale)
        kv = _mem_kv(mem2, lw["mem_norm"], lw["xa_wkv"]).reshape(b, m, 2 * d)
        x2 = _xattn(x2.reshape(b, s, d), lw["xa_norm"], lw["xa_wq"], kv, lw["xa_wo"]).reshape(b * s, d)
        x2 = _ffn(x2, lw["ffn2_norm"], lw["ffn2_wg"], lw["ffn2_wu"], lw["ffn2_wo"], w["final_norm"],
                  final_norm=(l == DEPTH - 1))
    return x2.reshape(b, s, d)


def kernel(x_prompt, x_sample, mem_prompt, mem_sample, ffn1_norm, ffn1_w_in, ffn1_w_out, mix_norm, mix_w_in, fourier_w, ret_decay_fwd, ret_decay_bwd, ret_w_out, mix_w_out, xa_norm, mem_norm, xa_wq, xa_wkv, xa_wo, ffn2_norm, ffn2_w_in, ffn2_w_out, final_norm):
    bf = lambda a: a.astype(BF16)
    row = lambda a: a.reshape(1, -1)
    w = {"final_norm": row(final_norm)}
    for l in range(DEPTH):
        dec = jnp.stack([ret_decay_fwd[l], ret_decay_bwd[l]])
        w[l] = {
            "ffn1_norm": row(ffn1_norm[l]),
            "ffn1_wg": bf(ffn1_w_in[l][:, :D_FF]), "ffn1_wu": bf(ffn1_w_in[l][:, D_FF:]),
            "ffn1_wo": bf(ffn1_w_out[l]),
            "mix_norm": row(mix_norm[l]), "mix_w_in": bf(mix_w_in[l]),
            "fourier_w": bf(fourier_w[l]), "ret_w_out": bf(ret_w_out[l]), "mix_w_out": bf(mix_w_out[l]),
            "dec": jnp.broadcast_to(dec[:, :, None, None], (2, RET_HEADS, 1, RET_DV)),
            "xa_norm": row(xa_norm[l]), "mem_norm": row(mem_norm[l]),
            "xa_wq": bf(xa_wq[l]), "xa_wkv": bf(xa_wkv[l]), "xa_wo": bf(xa_wo[l]),
            "ffn2_norm": row(ffn2_norm[l]),
            "ffn2_wg": bf(ffn2_w_in[l][:, :D_FF]), "ffn2_wu": bf(ffn2_w_in[l][:, D_FF:]),
            "ffn2_wo": bf(ffn2_w_out[l]),
        }
    bd = _channel_dft_matrix()
    y_prompt = _trunk(x_prompt, mem_prompt, w, bd)
    y_sample = _trunk(x_sample, mem_sample, w, bd)
    return (y_prompt, y_sample)
```

```python
import functools
import math

import jax
import jax.numpy as jnp
from jax import lax
from jax.experimental import pallas as pl
from jax.experimental.pallas import tpu as pltpu

F32 = jnp.float32
BF16 = jnp.bfloat16

D_MODEL = 1024
DEPTH = 4
D_FF = 2816
FOURIER_GROUP_DIM = 128
FOURIER_WIDTH = 512
RET_HEADS = 4
RET_DK = 128
RET_DV = 256
RET_QK_WIDTH = RET_HEADS * RET_DK
RET_V_WIDTH = RET_HEADS * RET_DV
XA_HEADS = 4
XA_HEAD_DIM = D_MODEL // XA_HEADS
ROPE_BASE = 10000.0
EPS = 1e-6

_C_F = (0, FOURIER_WIDTH)
_C_Q = (_C_F[1], _C_F[1] + RET_QK_WIDTH)
_C_K = (_C_Q[1], _C_Q[1] + RET_QK_WIDTH)
_C_V = (_C_K[1], _C_K[1] + RET_V_WIDTH)
_C_G = (_C_V[1], _C_V[1] + RET_V_WIDTH)
_C_GATES = (_C_G[1], _C_G[1] + 2 * D_MODEL)

_V7X_VMEM_BYTES = 64 * 1024 * 1024
_VMEM_LIMIT = _V7X_VMEM_BYTES - 8 * 1024 * 1024
_V7X_LANES = 128
_V7X_SUBLANES = 8

_RET_CHUNK = 256
_DFT_TWO_STAGE_MIN = 16384
_DFT_N1 = 128


def _params(n_axes):
    return pltpu.CompilerParams(
        dimension_semantics=("arbitrary",) * n_axes, vmem_limit_bytes=_VMEM_LIMIT)


def _resident(shape):
    nd = len(shape)
    return pl.BlockSpec(shape, lambda *_: (0,) * nd, pipeline_mode=pl.Buffered(1))


def _rmsnorm(x, g):
    ms = jnp.mean(x * x, axis=-1, keepdims=True)
    return x * lax.rsqrt(ms + EPS) * g


def _sigmoid(x):
    return 1.0 / (1.0 + jnp.exp(-x))


def _log_sigmoid(x):
    return -(jnp.maximum(-x, 0.0) + jnp.log1p(jnp.exp(-jnp.abs(x))))


def _dot(a, b):
    return jnp.dot(a, b, preferred_element_type=F32)


def _dot_nt(a, b):
    return lax.dot_general(a, b, (((1,), (1,)), ((), ())), preferred_element_type=F32)


def _dot_tn(a, b):
    return lax.dot_general(a, b, (((0,), (0,)), ((), ())), preferred_element_type=F32)


def _row_slices(total, rows):
    return [slice(r * rows, (r + 1) * rows) for r in range(total // rows)]


def _cast_specs(stacked, layer, nsteps, step_of):
    in_specs, out_specs, out_shapes = [], [], []
    for a in stacked:
        _, nrows, ncols = a.shape
        rc = nrows // nsteps
        assert rc * nsteps == nrows and rc % (2 * _V7X_SUBLANES) == 0, (a.shape, nsteps)
        in_specs.append(pl.BlockSpec((None, rc, ncols), lambda *g: (layer, step_of(*g), 0)))
        out_specs.append(pl.BlockSpec((rc, ncols), lambda *g: (step_of(*g), 0)))
        out_shapes.append(jax.ShapeDtypeStruct((nrows, ncols), BF16))
    return in_specs, out_specs, out_shapes


def _cast_chunks(src_refs, dst_refs):
    for src, dst in zip(src_refs, dst_refs):
        dst[...] = src[...].astype(BF16)


def _ffn_kernel(x_ref, g_ref, wg_ref, wu_ref, wo_ref, fg_ref, *rest, rows, tf, final_norm):
    n_cast = (len(rest) - 1) // 2
    o_ref = rest[n_cast]
    _cast_chunks(rest[:n_cast], rest[n_cast + 1:])
    f = wg_ref.shape[1]
    for sl in _row_slices(x_ref.shape[0], rows):
        x = x_ref[sl, :]
        xn = _rmsnorm(x, g_ref[...]).astype(BF16)
        acc = None
        for c in range(f // tf):
            cs = slice(c * tf, (c + 1) * tf)
            gate = _dot(xn, wg_ref[:, cs])
            up = _dot(xn, wu_ref[:, cs])
            act = (gate * _sigmoid(gate) * up).astype(BF16)
            part = _dot(act, wo_ref[cs, :])
            acc = part if acc is None else acc + part
        y = x + 0.5 * acc
        if final_norm:
            y = _rmsnorm(y, fg_ref[...])
        o_ref[sl, :] = y


def _ffn(x2, gain, w_in, w_out, final_gain, *, final_norm, cast=(), cast_layer=0,
         tm=1024, rows=256, tf=1408):
    t, d = x2.shape
    f = w_out.shape[0]
    nsteps = t // tm
    c_in, c_out, c_shapes = _cast_specs(cast, cast_layer, nsteps, lambda i: i)
    outs = pl.pallas_call(
        functools.partial(_ffn_kernel, rows=rows, tf=tf, final_norm=final_norm),
        grid=(nsteps,),
        in_specs=[
            pl.BlockSpec((tm, d), lambda i: (i, 0)),
            _resident((1, d)),
            pl.BlockSpec((d, f), lambda i: (0, 0), pipeline_mode=pl.Buffered(1)),
            pl.BlockSpec((d, f), lambda i: (0, 1), pipeline_mode=pl.Buffered(1)),
            _resident((f, d)), _resident((1, d)),
        ] + c_in,
        out_specs=[pl.BlockSpec((tm, d), lambda i: (i, 0))] + c_out,
        out_shape=[jax.ShapeDtypeStruct((t, d), F32)] + c_shapes,
        compiler_params=_params(1),
        name="ffn",
    )(x2, gain, w_in, w_in, w_out, final_gain, *cast)
    return outs[0], list(outs[1:])


def _mix_in_kernel(x_ref, g_ref, w_ref, cos_ref, sin_ref,
                   hf_ref, q_ref, k_ref, v_ref, sg_ref, gt_ref, *, rows):
    def rotary(hx, out_ref, sl, scale):
        cosf = cos_ref[sl, :]
        sinf = sin_ref[sl, :]
        for h in range(RET_HEADS):
            blk = hx[:, h * RET_DK:(h + 1) * RET_DK]
            rot = blk * cosf + pltpu.roll(blk, RET_DK // 2, 1) * sinf
            if scale != 1.0:
                rot = rot * scale
            out_ref[0, sl, h * RET_DK:(h + 1) * RET_DK] = rot.astype(out_ref.dtype)

    for sl in _row_slices(x_ref.shape[1], rows):
        xn = _rmsnorm(x_ref[0, sl, :], g_ref[...]).astype(BF16)

        def proj(cols):
            return _dot(xn, w_ref[:, cols[0]:cols[1]])

        hf_ref[0, sl, :] = proj(_C_F).astype(hf_ref.dtype)
        rotary(proj(_C_Q), q_ref, sl, 1.0)
        rotary(proj(_C_K), k_ref, sl, RET_DK ** -0.5)
        v_ref[0, sl, :] = proj(_C_V).astype(v_ref.dtype)
        hg = proj(_C_G)
        sg_ref[0, sl, :] = (hg * _sigmoid(hg)).astype(sg_ref.dtype)
        gt_ref[0, sl, :] = _sigmoid(proj(_C_GATES)).astype(gt_ref.dtype)


def _mix_in(x3, gain, w, cosf, sinf, *, hf_dtype, tm=1024, rows=256):
    b, s, d = x3.shape
    n = w.shape[1]

    def tok(width):
        return pl.BlockSpec((1, tm, width), lambda i, j: (i, j, 0))

    def out(width, dtype=BF16):
        return jax.ShapeDtypeStruct((b, s, width), dtype)

    return pl.pallas_call(
        functools.partial(_mix_in_kernel, rows=rows),
        grid=(b, s // tm),
        in_specs=[
            tok(d), _resident((1, d)), _resident((d, n)),
            pl.BlockSpec((tm, RET_DK), lambda i, j: (j, 0)),
            pl.BlockSpec((tm, RET_DK), lambda i, j: (j, 0)),
        ],
        out_specs=[tok(FOURIER_WIDTH), tok(RET_QK_WIDTH), tok(RET_QK_WIDTH),
                   tok(RET_V_WIDTH), tok(RET_V_WIDTH), tok(2 * D_MODEL)],
        out_shape=[out(FOURIER_WIDTH, hf_dtype), out(RET_QK_WIDTH), out(RET_QK_WIDTH),
                   out(RET_V_WIDTH), out(RET_V_WIDTH), out(2 * D_MODEL)],
        compiler_params=_params(2),
        name="mix_in",
    )(x3, gain, w, cosf, sinf)


def _dft_direct_kernel(a_ref, f_ref, xr_ref, xi_ref):
    s = a_ref.shape[1]
    res = _dot(f_ref[...], a_ref[0])
    xr_ref[0] = res[:s].astype(BF16)
    xi_ref[0] = res[s:].astype(BF16)


def _dft_direct(hf, fmat, *, bw=256):
    b, s, c = hf.shape
    spec = pl.BlockSpec((1, s, bw), lambda i, j: (i, 0, j))
    return pl.pallas_call(
        _dft_direct_kernel,
        grid=(b, c // bw),
        in_specs=[spec, _resident((2 * s, s))],
        out_specs=[spec, spec],
        out_shape=[jax.ShapeDtypeStruct((b, s, c), BF16)] * 2,
        compiler_params=_params(2),
        name="dft_direct",
    )(hf, fmat)


def _lane_blocks(c):
    return [slice(q * _V7X_LANES, (q + 1) * _V7X_LANES) for q in range(c // _V7X_LANES)]


def _dft_stage1_kernel(a_ref, f_ref, tc_ref, ts_ref, tr_ref, ti_ref, a2, tr2, ti2):
    n1 = a_ref.shape[1]
    slab = (n1, _V7X_SUBLANES, _V7X_LANES)
    for q, ls in enumerate(_lane_blocks(a_ref.shape[-1])):
        a2[q] = a_ref[0, :, :, ls].reshape(a2.shape[1:])
        for j in range(_V7X_SUBLANES):
            pick = pl.ds(j, n1, stride=_V7X_SUBLANES)
            res = _dot(f_ref[...], a2[q, pick, :].astype(BF16))
            ur, ui = res[:n1], res[n1:]
            cw, sw = tc_ref[j], ts_ref[j]
            tr2[q, pick, :] = ur * cw + ui * sw
            ti2[q, pick, :] = ui * cw - ur * sw
        tr_ref[0, :, :, ls] = tr2[q].reshape(slab)
        ti_ref[0, :, :, ls] = ti2[q].reshape(slab)


def _dft_stage1(hf, f1, twc, tws, *, n1):
    b, s, c = hf.shape
    n2 = s // n1
    blk = pl.BlockSpec((1, n1, _V7X_SUBLANES, c), lambda i, j: (i, 0, j, 0))
    tw = pl.BlockSpec((_V7X_SUBLANES, n1, _V7X_LANES), lambda i, j: (j, 0, 0))
    rows = pltpu.VMEM((c // _V7X_LANES, n1 * _V7X_SUBLANES, _V7X_LANES), F32)
    return pl.pallas_call(
        _dft_stage1_kernel,
        grid=(b, n2 // _V7X_SUBLANES),
        in_specs=[blk, _resident((2 * n1, n1)), tw, tw],
        out_specs=[blk, blk],
        out_shape=[jax.ShapeDtypeStruct((b, n1, n2, c), F32)] * 2,
        scratch_shapes=[rows, rows, rows],
        compiler_params=_params(2),
        name="dft_stage1",
    )(hf.reshape(b, n1, n2, c), f1, twc, tws)


def _dft_stage2_kernel(tr_ref, ti_ref, f_ref, xr_ref, xi_ref, xr2, xi2):
    n2 = tr_ref.shape[2]
    slab = (n2, _V7X_SUBLANES, _V7X_LANES)
    for q, ls in enumerate(_lane_blocks(tr_ref.shape[-1])):
        for j in range(_V7X_SUBLANES):
            t = jnp.concatenate([tr_ref[0, j, :, ls], ti_ref[0, j, :, ls]], axis=0).astype(BF16)
            res = _dot(f_ref[...], t)
            pick = pl.ds(j, n2, stride=_V7X_SUBLANES)
            xr2[q, pick, :] = res[:n2]
            xi2[q, pick, :] = res[n2:]
        xr_ref[0, :, :, ls] = xr2[q].reshape(slab)
        xi_ref[0, :, :, ls] = xi2[q].reshape(slab)


def _dft_stage2(tr, ti, f2):
    b, n1, n2, c = tr.shape
    inb = pl.BlockSpec((1, _V7X_SUBLANES, n2, c), lambda i, j: (i, j, 0, 0))
    outb = pl.BlockSpec((1, n2, _V7X_SUBLANES, c), lambda i, j: (i, 0, j, 0))
    rows = pltpu.VMEM((c // _V7X_LANES, n2 * _V7X_SUBLANES, _V7X_LANES), F32)
    xr, xi = pl.pallas_call(
        _dft_stage2_kernel,
        grid=(b, n1 // _V7X_SUBLANES),
        in_specs=[inb, inb, _resident((2 * n2, 2 * n2))],
        out_specs=[outb, outb],
        out_shape=[jax.ShapeDtypeStruct((b, n2, n1, c), F32)] * 2,
        scratch_shapes=[rows, rows],
        compiler_params=_params(2),
        name="dft_stage2",
    )(tr, ti, f2)
    return xr.reshape(b, n2 * n1, c), xi.reshape(b, n2 * n1, c)


def _ret_bwd_kernel(dec_ref, k_ref, v_ref, rb_ref, zb_ref, gc_ref, st_ref, *, cc, nch):
    b = pl.program_id(0)
    t = pl.program_id(1)

    @pl.when((b == 0) & (t == 0))
    def _():
        rowi = lax.broadcasted_iota(jnp.int32, (cc, RET_DV), 0).astype(F32)
        for h in range(RET_HEADS):
            lgb = _log_sigmoid(dec_ref[1, h])
            zb_ref[h] = jnp.exp(lgb * rowi)[:, :RET_DK]
            gc_ref[h] = jnp.exp(lgb * float(cc))

    @pl.when(t == 0)
    def _():
        st_ref[...] = jnp.zeros_like(st_ref)

    for h in range(RET_HEADS):
        st = st_ref[h]
        for c in reversed(range(nch)):
            rs = slice(c * cc, (c + 1) * cc)
            rb_ref[0, c, h * RET_DK:(h + 1) * RET_DK, :] = st.astype(BF16)
            kh = k_ref[0, rs, h * RET_DK:(h + 1) * RET_DK]
            vh = v_ref[0, rs, h * RET_DV:(h + 1) * RET_DV]
            kz = (kh.astype(F32) * zb_ref[h]).astype(BF16)
            st = gc_ref[h] * st + _dot_tn(kz, vh)
        st_ref[h] = st


def _ret_bwd(dec, k, v, *, cc, nch=4):
    b, s, _ = k.shape
    ns = s // (cc * nch)
    return pl.pallas_call(
        functools.partial(_ret_bwd_kernel, cc=cc, nch=nch),
        grid=(b, ns),
        in_specs=[
            _resident(dec.shape),
            pl.BlockSpec((1, cc * nch, RET_QK_WIDTH), lambda i, t: (i, ns - 1 - t, 0)),
            pl.BlockSpec((1, cc * nch, RET_V_WIDTH), lambda i, t: (i, ns - 1 - t, 0)),
        ],
        out_specs=pl.BlockSpec((1, nch, RET_QK_WIDTH, RET_DV), lambda i, t: (i, ns - 1 - t, 0, 0)),
        out_shape=jax.ShapeDtypeStruct((b, s // cc, RET_QK_WIDTH, RET_DV), BF16),
        scratch_shapes=[
            pltpu.VMEM((RET_HEADS, cc, RET_DK), F32),
            pltpu.VMEM((RET_HEADS, 1, RET_DV), F32),
            pltpu.VMEM((RET_HEADS, RET_DK, RET_DV), F32),
        ],
        compiler_params=_params(2),
        name="ret_bwd",
    )(dec, k, v)


def _ret_fwd_kernel(dec_ref, q_ref, k_ref, v_ref, sg_ref, rb_ref, o_ref,
                    dm_ref, xif_ref, xib_ref, zf_ref, gc_ref, st_ref, *, cc, nch):
    b = pl.program_id(0)
    j = pl.program_id(1)

    @pl.when((b == 0) & (j == 0))
    def _():
        row = lax.broadcasted_iota(jnp.int32, (cc, cc), 0)
        col = lax.broadcasted_iota(jnp.int32, (cc, cc), 1)
        diff = (row - col).astype(F32)
        rowi = lax.broadcasted_iota(jnp.int32, (cc, RET_DV), 0).astype(F32)
        for h in range(RET_HEADS):
            lgf = _log_sigmoid(dec_ref[0, h])
            lgb = _log_sigmoid(dec_ref[1, h])
            dm_ref[h] = jnp.where(diff >= 0.0,
                                  jnp.exp(lgf * jnp.maximum(diff, 0.0)),
                                  jnp.exp(lgb * jnp.maximum(-diff, 0.0)))
            xif_ref[h] = jnp.exp(lgf * (rowi + 1.0))[:, :RET_DK]
            xib_ref[h] = jnp.exp(lgb * (float(cc) - rowi))[:, :RET_DK]
            zf_ref[h] = jnp.exp(lgf * (float(cc) - 1.0 - rowi))[:, :RET_DK]
            gc_ref[h] = jnp.exp(lgf * float(cc))

    @pl.when(j == 0)
    def _():
        st_ref[...] = jnp.zeros_like(st_ref)

    for h in range(RET_HEADS):
        st = st_ref[h]
        for c in range(nch):
            rs = slice(c * cc, (c + 1) * cc)
            qh = q_ref[0, rs, h * RET_DK:(h + 1) * RET_DK]
            kh = k_ref[0, rs, h * RET_DK:(h + 1) * RET_DK]
            vh = v_ref[0, rs, h * RET_DV:(h + 1) * RET_DV]
            p = (_dot_nt(qh, kh) * dm_ref[h]).astype(BF16)
            qf32 = qh.astype(F32)
            qf = (qf32 * xif_ref[h]).astype(BF16)
            qb = (qf32 * xib_ref[h]).astype(BF16)
            lhs = jnp.concatenate([p, qf, qb], axis=1)
            rhs = jnp.concatenate(
                [vh, st.astype(BF16), rb_ref[0, c, h * RET_DK:(h + 1) * RET_DK, :]], axis=0)
            y = _dot(lhs, rhs)
            kz = (kh.astype(F32) * zf_ref[h]).astype(BF16)
            st = gc_ref[h] * st + _dot_tn(kz, vh)
            ms = jnp.mean(y * y, axis=-1, keepdims=True)
            yn = y * lax.rsqrt(ms + EPS)
            sg = sg_ref[0, rs, h * RET_DV:(h + 1) * RET_DV].astype(F32)
            o_ref[0, rs, h * RET_DV:(h + 1) * RET_DV] = (sg * yn).astype(BF16)
        st_ref[h] = st


def _ret_fwd(dec, q, k, v, sg, rb, *, cc, nch=4):
    b, s, _ = q.shape
    ns = s // (cc * nch)

    def tok(width):
        return pl.BlockSpec((1, cc * nch, width), lambda i, j: (i, j, 0))

    return pl.pallas_call(
        functools.partial(_ret_fwd_kernel, cc=cc, nch=nch),
        grid=(b, ns),
        in_specs=[
            _resident(dec.shape),
            tok(RET_QK_WIDTH), tok(RET_QK_WIDTH), tok(RET_V_WIDTH), tok(RET_V_WIDTH),
            pl.BlockSpec((1, nch, RET_QK_WIDTH, RET_DV), lambda i, j: (i, j, 0, 0)),
        ],
        out_specs=tok(RET_V_WIDTH),
        out_shape=jax.ShapeDtypeStruct((b, s, RET_V_WIDTH), BF16),
        scratch_shapes=[
            pltpu.VMEM((RET_HEADS, cc, cc), F32),
            pltpu.VMEM((RET_HEADS, cc, RET_DK), F32),
            pltpu.VMEM((RET_HEADS, cc, RET_DK), F32),
            pltpu.VMEM((RET_HEADS, cc, RET_DK), F32),
            pltpu.VMEM((RET_HEADS, 1, RET_DV), F32),
            pltpu.VMEM((RET_HEADS, RET_DK, RET_DV), F32),
        ],
        compiler_params=_params(2),
        name="ret_fwd",
    )(dec, q, k, v, sg, rb)


def _mix_out_kernel(x_ref, xr_ref, xi_ref, yr_ref, gt_ref, cs_ref, fw_ref, rw_ref, mw_ref,
                    o_ref, *, scale, rows):
    d = x_ref.shape[-1]
    for sl in _row_slices(x_ref.shape[0], rows):
        groups = [
            _dot(jnp.concatenate([xr_ref[sl, ls], xi_ref[sl, ls]], axis=1).astype(BF16), cs_ref[...])
            for ls in _lane_blocks(xr_ref.shape[-1])]
        yf = jnp.concatenate(groups, axis=1) * scale
        ya = _dot(yf.astype(BF16), fw_ref[...])
        yb = _dot(yr_ref[sl, :], rw_ref[...])
        ga = gt_ref[sl, :d].astype(F32)
        gb = gt_ref[sl, d:].astype(F32)
        merged = (ga * ya + gb * yb).astype(BF16)
        o_ref[sl, :] = x_ref[sl, :] + _dot(merged, mw_ref[...])


def _mix_out(x2, xr, xi, yr, gt, cs, fw, rw, mw, *, scale, tm=1024, rows=512):
    t, d = x2.shape

    def tok(width):
        return pl.BlockSpec((tm, width), lambda i: (i, 0))

    return pl.pallas_call(
        functools.partial(_mix_out_kernel, scale=scale, rows=rows),
        grid=(t // tm,),
        in_specs=[tok(d), tok(FOURIER_WIDTH), tok(FOURIER_WIDTH), tok(RET_V_WIDTH), tok(2 * d),
                  _resident(cs.shape), _resident(fw.shape), _resident(rw.shape), _resident(mw.shape)],
        out_specs=tok(d),
        out_shape=jax.ShapeDtypeStruct((t, d), F32),
        compiler_params=_params(1),
        name="mix_out",
    )(x2, xr, xi, yr, gt, cs, fw, rw, mw)


def _mem_kv_kernel(m_ref, g_ref, w_ref, o_ref):
    mn = _rmsnorm(m_ref[...], g_ref[...]).astype(BF16)
    o_ref[...] = _dot(mn, w_ref[...]).astype(BF16)


def _mem_kv(mem2, gain, w, *, tm=256):
    t, d = mem2.shape
    n = w.shape[1]
    return pl.pallas_call(
        _mem_kv_kernel,
        grid=(t // tm,),
        in_specs=[pl.BlockSpec((tm, d), lambda i: (i, 0)), _resident((1, d)), _resident((d, n))],
        out_specs=pl.BlockSpec((tm, n), lambda i: (i, 0)),
        out_shape=jax.ShapeDtypeStruct((t, n), BF16),
        compiler_params=_params(1),
        name="mem_kv",
    )(mem2, gain, w)


def _xattn_kernel(x_ref, g_ref, wq_ref, kv_ref, wo_ref, *rest, rows):
    n_cast = (len(rest) - 1) // 2
    o_ref = rest[n_cast]
    _cast_chunks(rest[:n_cast], rest[n_cast + 1:])
    d = x_ref.shape[-1]
    heads = [(h * XA_HEAD_DIM, (h + 1) * XA_HEAD_DIM) for h in range(XA_HEADS)]
    sls = _row_slices(x_ref.shape[1], rows)
    xs = [x_ref[0, sl, :] for sl in sls]
    qs = [_dot(_rmsnorm(x, g_ref[...]).astype(BF16), wq_ref[...]).astype(BF16) for x in xs]
    scores = [[_dot_nt(q[:, lo:hi], kv_ref[0, :, lo:hi]) * (XA_HEAD_DIM ** -0.5) for lo, hi in heads]
              for q in qs]
    probs = []
    for per_head in scores:
        row = []
        for s in per_head:
            e = jnp.exp(s - jnp.max(s, axis=-1, keepdims=True))
            row.append((e / jnp.sum(e, axis=-1, keepdims=True)).astype(BF16))
        probs.append(row)
    outs = [[_dot(p, kv_ref[0, :, d + lo:d + hi]).astype(BF16) for p, (lo, hi) in zip(row, heads)]
            for row in probs]
    for sl, x, per_head in zip(sls, xs, outs):
        o_ref[0, sl, :] = x + _dot(jnp.concatenate(per_head, axis=-1), wo_ref[...])


def _xattn(x3, gain, wq, kv, wo, *, cast=(), cast_layer=0, tm=1024, rows=512):
    b, s, d = x3.shape
    m = kv.shape[1]
    per_seq = s // tm
    tok = pl.BlockSpec((1, tm, d), lambda i, j: (i, j, 0))
    c_in, c_out, c_shapes = _cast_specs(cast, cast_layer, b * per_seq, lambda i, j: i * per_seq + j)
    outs = pl.pallas_call(
        functools.partial(_xattn_kernel, rows=rows),
        grid=(b, per_seq),
        in_specs=[tok, _resident((1, d)), _resident((d, d)),
                  pl.BlockSpec((1, m, 2 * d), lambda i, j: (i, 0, 0)),
                  _resident((d, d))] + c_in,
        out_specs=[tok] + c_out,
        out_shape=[jax.ShapeDtypeStruct((b, s, d), F32)] + c_shapes,
        compiler_params=_params(2),
        name="xattn",
    )(x3, gain, wq, kv, wo, *cast)
    return outs[0], list(outs[1:])


def _angle(num, n):
    return (2.0 * math.pi / n) * (num % n).astype(F32)


_DFT_TABLE_SPLIT = 32


def _dft_cos_sin(n):
    i = jnp.arange(n, dtype=jnp.int32)
    if n <= _DFT_TABLE_SPLIT * _V7X_LANES:
        ang = _angle(i[:, None] * i[None, :], n)
        return jnp.cos(ang), jnp.sin(ang)
    na = n // _DFT_TABLE_SPLIT
    ang_a = _angle(i[:, None] * jnp.arange(na, dtype=jnp.int32)[None, :], na)[:, :, None]
    ang_b = _angle(i[:, None] * jnp.arange(_DFT_TABLE_SPLIT, dtype=jnp.int32)[None, :], n)[:, None, :]
    ca, sa, cb, sb = lax.optimization_barrier(
        (jnp.cos(ang_a), jnp.sin(ang_a), jnp.cos(ang_b), jnp.sin(ang_b)))
    return (ca * cb - sa * sb).reshape(n, n), (sa * cb + ca * sb).reshape(n, n)


def _rotary_tables(s):
    half = RET_DK // 2
    inv = 1.0 / (ROPE_BASE ** (jnp.arange(half, dtype=F32) * 2.0 / RET_DK))
    ang = jnp.arange(s, dtype=F32)[:, None] * inv[None, :]
    cos, sin = jnp.cos(ang), jnp.sin(ang)
    return jnp.concatenate([cos, cos], -1), jnp.concatenate([-sin, sin], -1)


def _seq_tables(s):
    tabs = {"rot": _rotary_tables(s)}
    if s >= _DFT_TWO_STAGE_MIN:
        n1, n2 = _DFT_N1, s // _DFT_N1
        c1, s1 = _dft_cos_sin(n1)
        tabs["f1"] = jnp.concatenate([c1, -s1], 0).astype(BF16)
        c2, s2 = _dft_cos_sin(n2)
        tabs["f2"] = jnp.concatenate(
            [jnp.concatenate([c2, s2], 1), jnp.concatenate([-s2, c2], 1)], 0).astype(BF16)
        ang = _angle(jnp.arange(n2, dtype=jnp.int32)[:, None] * jnp.arange(n1, dtype=jnp.int32)[None, :], s)
        lane = (n2, n1, _V7X_LANES)
        tabs["tw"] = (jnp.broadcast_to(jnp.cos(ang)[:, :, None], lane),
                      jnp.broadcast_to(jnp.sin(ang)[:, :, None], lane))
    else:
        c, sn = _dft_cos_sin(s)
        tabs["f"] = jnp.concatenate([c, -sn], 0).astype(BF16)
    return tabs


def _channel_dft_matrix():
    c, s = _dft_cos_sin(FOURIER_GROUP_DIM)
    return jnp.concatenate([c, s], 0).astype(BF16)


_CAST_BY_FFN1 = ("ffn1_w_in", "ffn1_w_out")
_CAST_BY_XATTN = ("mix_w_in", "fourier_w", "ret_w_out", "mix_w_out", "xa_wq", "xa_wkv", "xa_wo")
_CAST_BY_FFN2 = ("ffn2_w_in", "ffn2_w_out")


def _trunk(x, mem, small, wb, bd, stacked=None):
    b, s, d = x.shape
    m = mem.shape[1]
    tabs = _seq_tables(s)
    cosf, sinf = tabs["rot"]
    two_stage = "f" not in tabs
    scale = float((s * FOURIER_GROUP_DIM) ** -0.5)
    x2 = x.reshape(b * s, d)
    mem2 = mem.reshape(b * m, d)
    for l in range(DEPTH):
        sm, lw = small[l], wb[l]
        ahead = stacked is not None and l + 1 < DEPTH

        def cast_args(names):
            if not ahead:
                return {}
            return {"cast": [stacked[n] for n in names], "cast_layer": l + 1}

        def keep(names, copies):
            if ahead:
                wb.setdefault(l + 1, {}).update(zip(names, copies))

        x2, copies = _ffn(x2, sm["ffn1_norm"], lw["ffn1_w_in"], lw["ffn1_w_out"], small["final_norm"],
                          final_norm=False, **cast_args(_CAST_BY_FFN1))
        keep(_CAST_BY_FFN1, copies)
        hf, q, k, v, sg, gt = _mix_in(x2.reshape(b, s, d), sm["mix_norm"], lw["mix_w_in"], cosf, sinf,
                                      hf_dtype=F32 if two_stage else BF16)
        if two_stage:
            tr, ti = _dft_stage1(hf, tabs["f1"], *tabs["tw"], n1=_DFT_N1)
            xr, xi = _dft_stage2(tr, ti, tabs["f2"])
        else:
            xr, xi = _dft_direct(hf, tabs["f"])
        rb = _ret_bwd(sm["dec"], k, v, cc=_RET_CHUNK)
        yr = _ret_fwd(sm["dec"], q, k, v, sg, rb, cc=_RET_CHUNK)
        x2 = _mix_out(x2, xr.reshape(b * s, -1), xi.reshape(b * s, -1), yr.reshape(b * s, -1),
                      gt.reshape(b * s, -1), bd, lw["fourier_w"], lw["ret_w_out"],
                      lw["mix_w_out"], scale=scale)
        kv = _mem_kv(mem2, sm["mem_norm"], lw["xa_wkv"]).reshape(b, m, 2 * d)
        x3, copies = _xattn(x2.reshape(b, s, d), sm["xa_norm"], lw["xa_wq"], kv, lw["xa_wo"],
                            **cast_args(_CAST_BY_XATTN))
        keep(_CAST_BY_XATTN, copies)
        x2, copies = _ffn(x3.reshape(b * s, d), sm["ffn2_norm"], lw["ffn2_w_in"], lw["ffn2_w_out"],
                          small["final_norm"], final_norm=(l == DEPTH - 1), **cast_args(_CAST_BY_FFN2))
        keep(_CAST_BY_FFN2, copies)
    return x2.reshape(b, s, d)


def kernel(x_prompt, x_sample, mem_prompt, mem_sample, ffn1_norm, ffn1_w_in, ffn1_w_out, mix_norm, mix_w_in, fourier_w, ret_decay_fwd, ret_decay_bwd, ret_w_out, mix_w_out, xa_norm, mem_norm, xa_wq, xa_wkv, xa_wo, ffn2_norm, ffn2_w_in, ffn2_w_out, final_norm):
    row = lambda a: a.reshape(1, -1)
    small = {"final_norm": row(final_norm)}
    for l in range(DEPTH):
        dec = jnp.stack([ret_decay_fwd[l], ret_decay_bwd[l]])
        small[l] = {
            "ffn1_norm": row(ffn1_norm[l]), "mix_norm": row(mix_norm[l]), "xa_norm": row(xa_norm[l]),
            "mem_norm": row(mem_norm[l]), "ffn2_norm": row(ffn2_norm[l]),
            "dec": jnp.broadcast_to(dec[:, :, None, None], (2, RET_HEADS, 1, RET_DV)),
        }
    stacked = {
        "ffn1_w_in": ffn1_w_in, "ffn1_w_out": ffn1_w_out, "mix_w_in": mix_w_in, "fourier_w": fourier_w,
        "ret_w_out": ret_w_out, "mix_w_out": mix_w_out, "xa_wq": xa_wq, "xa_wkv": xa_wkv, "xa_wo": xa_wo,
        "ffn2_w_in": ffn2_w_in, "ffn2_w_out": ffn2_w_out,
    }
    wb = {0: {name: a[0].astype(BF16) for name, a in stacked.items()}}
    bd = _channel_dft_matrix()
    y_prompt = _trunk(x_prompt, mem_prompt, small, wb, bd, stacked)
    y_sample = _trunk(x_sample, mem_sample, small, wb, bd)
    return (y_prompt, y_sample)
```

```python
import functools
import math

import jax
import jax.numpy as jnp
from jax import lax
from jax.experimental import pallas as pl
from jax.experimental.pallas import tpu as pltpu

F32 = jnp.float32
BF16 = jnp.bfloat16

D_MODEL = 1024
DEPTH = 4
D_FF = 2816
FOURIER_GROUP_DIM = 128
FOURIER_WIDTH = 512
RET_HEADS = 4
RET_DK = 128
RET_DV = 256
RET_QK_WIDTH = RET_HEADS * RET_DK
RET_V_WIDTH = RET_HEADS * RET_DV
XA_HEADS = 4
XA_HEAD_DIM = D_MODEL // XA_HEADS
ROPE_BASE = 10000.0
EPS = 1e-6

_C_F = (0, FOURIER_WIDTH)
_C_Q = (_C_F[1], _C_F[1] + RET_QK_WIDTH)
_C_K = (_C_Q[1], _C_Q[1] + RET_QK_WIDTH)
_C_V = (_C_K[1], _C_K[1] + RET_V_WIDTH)
_C_G = (_C_V[1], _C_V[1] + RET_V_WIDTH)
_C_GATES = (_C_G[1], _C_G[1] + 2 * D_MODEL)

_V7X_VMEM_BYTES = 64 * 1024 * 1024
_VMEM_LIMIT = _V7X_VMEM_BYTES - 8 * 1024 * 1024
_V7X_LANES = 128
_V7X_SUBLANES = 8

_RET_CHUNK = 256
_DFT_TWO_STAGE_MIN = 16384
_DFT_N1 = 128


def _params(n_axes):
    return pltpu.CompilerParams(
        dimension_semantics=("arbitrary",) * n_axes, vmem_limit_bytes=_VMEM_LIMIT)


def _resident(shape):
    nd = len(shape)
    return pl.BlockSpec(shape, lambda *_: (0,) * nd, pipeline_mode=pl.Buffered(1))


def _rmsnorm(x, g):
    ms = jnp.mean(x * x, axis=-1, keepdims=True)
    return x * lax.rsqrt(ms + EPS) * g


def _sigmoid(x):
    return 1.0 / (1.0 + jnp.exp(-x))


def _log_sigmoid(x):
    return -(jnp.maximum(-x, 0.0) + jnp.log1p(jnp.exp(-jnp.abs(x))))


def _dot(a, b):
    return jnp.dot(a, b, preferred_element_type=F32)


def _dot_nt(a, b):
    return lax.dot_general(a, b, (((1,), (1,)), ((), ())), preferred_element_type=F32)


def _dot_tn(a, b):
    return lax.dot_general(a, b, (((0,), (0,)), ((), ())), preferred_element_type=F32)


def _row_slices(total, rows):
    return [slice(r * rows, (r + 1) * rows) for r in range(total // rows)]


def _cast_specs(stacked, layer, nsteps, step_of):
    in_specs, out_specs, out_shapes = [], [], []
    for a in stacked:
        _, nrows, ncols = a.shape
        rc = nrows // nsteps
        assert rc * nsteps == nrows and rc % (2 * _V7X_SUBLANES) == 0, (a.shape, nsteps)
        in_specs.append(pl.BlockSpec((None, rc, ncols), lambda *g: (layer, step_of(*g), 0)))
        out_specs.append(pl.BlockSpec((rc, ncols), lambda *g: (step_of(*g), 0)))
        out_shapes.append(jax.ShapeDtypeStruct((nrows, ncols), BF16))
    return in_specs, out_specs, out_shapes


def _cast_chunks(src_refs, dst_refs):
    for src, dst in zip(src_refs, dst_refs):
        dst[...] = src[...].astype(BF16)


def _ffn_kernel(x_ref, g_ref, wg_ref, wu_ref, wo_ref, fg_ref, *rest, rows, tf, final_norm):
    n_cast = (len(rest) - 1) // 2
    o_ref = rest[n_cast]
    _cast_chunks(rest[:n_cast], rest[n_cast + 1:])
    f = wg_ref.shape[1]
    for sl in _row_slices(x_ref.shape[0], rows):
        x = x_ref[sl, :]
        xn = _rmsnorm(x, g_ref[...]).astype(BF16)
        acc = None
        for c in range(f // tf):
            cs = slice(c * tf, (c + 1) * tf)
            gate = _dot(xn, wg_ref[:, cs])
            up = _dot(xn, wu_ref[:, cs])
            act = (gate * _sigmoid(gate) * up).astype(BF16)
            part = _dot(act, wo_ref[cs, :])
            acc = part if acc is None else acc + part
        y = x + 0.5 * acc
        if final_norm:
            y = _rmsnorm(y, fg_ref[...])
        o_ref[sl, :] = y


def _ffn(x2, gain, w_in, w_out, final_gain, *, final_norm, cast=(), cast_layer=0,
         tm=1024, rows=256, tf=1408):
    t, d = x2.shape
    f = w_out.shape[0]
    nsteps = t // tm
    c_in, c_out, c_shapes = _cast_specs(cast, cast_layer, nsteps, lambda i: i)
    outs = pl.pallas_call(
        functools.partial(_ffn_kernel, rows=rows, tf=tf, final_norm=final_norm),
        grid=(nsteps,),
        in_specs=[
            pl.BlockSpec((tm, d), lambda i: (i, 0)),
            _resident((1, d)),
            pl.BlockSpec((d, f), lambda i: (0, 0), pipeline_mode=pl.Buffered(1)),
            pl.BlockSpec((d, f), lambda i: (0, 1), pipeline_mode=pl.Buffered(1)),
            _resident((f, d)), _resident((1, d)),
        ] + c_in,
        out_specs=[pl.BlockSpec((tm, d), lambda i: (i, 0))] + c_out,
        out_shape=[jax.ShapeDtypeStruct((t, d), F32)] + c_shapes,
        compiler_params=_params(1),
        name="ffn",
    )(x2, gain, w_in, w_in, w_out, final_gain, *cast)
    return outs[0], list(outs[1:])


def _mix_in_kernel(x_ref, g_ref, w_ref, cos_ref, sin_ref,
                   hf_ref, q_ref, k_ref, v_ref, sg_ref, gt_ref, *, rows):
    def rotary(hx, out_ref, sl, scale):
        cosf = cos_ref[sl, :]
        sinf = sin_ref[sl, :]
        for h in range(RET_HEADS):
            blk = hx[:, h * RET_DK:(h + 1) * RET_DK]
            rot = blk * cosf + pltpu.roll(blk, RET_DK // 2, 1) * sinf
            if scale != 1.0:
                rot = rot * scale
            out_ref[0, sl, h * RET_DK:(h + 1) * RET_DK] = rot.astype(out_ref.dtype)

    for sl in _row_slices(x_ref.shape[1], rows):
        xn = _rmsnorm(x_ref[0, sl, :], g_ref[...]).astype(BF16)

        def proj(cols):
            return _dot(xn, w_ref[:, cols[0]:cols[1]])

        hf_ref[0, sl, :] = proj(_C_F).astype(hf_ref.dtype)
        rotary(proj(_C_Q), q_ref, sl, 1.0)
        rotary(proj(_C_K), k_ref, sl, RET_DK ** -0.5)
        v_ref[0, sl, :] = proj(_C_V).astype(v_ref.dtype)
        hg = proj(_C_G)
        sg_ref[0, sl, :] = (hg * _sigmoid(hg)).astype(sg_ref.dtype)
        gt_ref[0, sl, :] = _sigmoid(proj(_C_GATES)).astype(gt_ref.dtype)


def _mix_in(x3, gain, w, cosf, sinf, *, hf_dtype, tm=1024, rows=256):
    b, s, d = x3.shape
    n = w.shape[1]

    def tok(width):
        return pl.BlockSpec((1, tm, width), lambda i, j: (i, j, 0))

    def out(width, dtype=BF16):
        return jax.ShapeDtypeStruct((b, s, width), dtype)

    return pl.pallas_call(
        functools.partial(_mix_in_kernel, rows=rows),
        grid=(b, s // tm),
        in_specs=[
            tok(d), _resident((1, d)), _resident((d, n)),
            pl.BlockSpec((tm, RET_DK), lambda i, j: (j, 0)),
            pl.BlockSpec((tm, RET_DK), lambda i, j: (j, 0)),
        ],
        out_specs=[tok(FOURIER_WIDTH), tok(RET_QK_WIDTH), tok(RET_QK_WIDTH),
                   tok(RET_V_WIDTH), tok(RET_V_WIDTH), tok(2 * D_MODEL)],
        out_shape=[out(FOURIER_WIDTH, hf_dtype), out(RET_QK_WIDTH), out(RET_QK_WIDTH),
                   out(RET_V_WIDTH), out(RET_V_WIDTH), out(2 * D_MODEL)],
        compiler_params=_params(2),
        name="mix_in",
    )(x3, gain, w, cosf, sinf)


def _dft_direct_kernel(a_ref, f_ref, xr_ref, xi_ref):
    s = a_ref.shape[1]
    res = _dot(f_ref[...], a_ref[0])
    xr_ref[0] = res[:s].astype(BF16)
    xi_ref[0] = res[s:].astype(BF16)


def _dft_direct(hf, fmat, *, bw=256):
    b, s, c = hf.shape
    spec = pl.BlockSpec((1, s, bw), lambda i, j: (i, 0, j))
    return pl.pallas_call(
        _dft_direct_kernel,
        grid=(b, c // bw),
        in_specs=[spec, _resident((2 * s, s))],
        out_specs=[spec, spec],
        out_shape=[jax.ShapeDtypeStruct((b, s, c), BF16)] * 2,
        compiler_params=_params(2),
        name="dft_direct",
    )(hf, fmat)


def _lane_blocks(c):
    return [slice(q * _V7X_LANES, (q + 1) * _V7X_LANES) for q in range(c // _V7X_LANES)]


def _pack_pair(re, im):
    hi = lax.bitcast_convert_type(re.astype(BF16).astype(F32), jnp.uint32)
    lo = lax.bitcast_convert_type(im.astype(BF16).astype(F32), jnp.uint32)
    return hi | (lo >> 16)


def _unpack_pair(w):
    re = lax.bitcast_convert_type(w & jnp.uint32(0xFFFF0000), F32)
    im = lax.bitcast_convert_type(w << 16, F32)
    return re, im


def _dft_stage1_kernel(a_ref, f_ref, tc_ref, ts_ref, t_ref, a2, t2):
    n1 = a_ref.shape[1]
    slab = (n1, _V7X_SUBLANES, _V7X_LANES)
    for q, ls in enumerate(_lane_blocks(a_ref.shape[-1])):
        a2[q] = a_ref[0, :, :, ls].reshape(a2.shape[1:])
        for j in range(_V7X_SUBLANES):
            pick = pl.ds(j, n1, stride=_V7X_SUBLANES)
            res = _dot(f_ref[...], a2[q, pick, :].astype(BF16))
            ur, ui = res[:n1], res[n1:]
            cw, sw = tc_ref[j], ts_ref[j]
            t2[q, pick, :] = _pack_pair(ur * cw + ui * sw, ui * cw - ur * sw)
        t_ref[0, :, :, ls] = t2[q].reshape(slab)


def _dft_stage1(hf, f1, twc, tws, *, n1):
    b, s, c = hf.shape
    n2 = s // n1
    blk = pl.BlockSpec((1, n1, _V7X_SUBLANES, c), lambda i, j: (i, 0, j, 0))
    tw = pl.BlockSpec((_V7X_SUBLANES, n1, _V7X_LANES), lambda i, j: (j, 0, 0))
    rows = (c // _V7X_LANES, n1 * _V7X_SUBLANES, _V7X_LANES)
    return pl.pallas_call(
        _dft_stage1_kernel,
        grid=(b, n2 // _V7X_SUBLANES),
        in_specs=[blk, _resident((2 * n1, n1)), tw, tw],
        out_specs=blk,
        out_shape=jax.ShapeDtypeStruct((b, n1, n2, c), jnp.uint32),
        scratch_shapes=[pltpu.VMEM(rows, F32), pltpu.VMEM(rows, jnp.uint32)],
        compiler_params=_params(2),
        name="dft_stage1",
    )(hf.reshape(b, n1, n2, c), f1, twc, tws)


def _dft_stage2_kernel(t_ref, f_ref, x_ref, x2):
    n2 = t_ref.shape[2]
    slab = (n2, _V7X_SUBLANES, _V7X_LANES)
    for q, ls in enumerate(_lane_blocks(t_ref.shape[-1])):
        for j in range(_V7X_SUBLANES):
            t = jnp.concatenate(_unpack_pair(t_ref[0, j, :, ls]), axis=0).astype(BF16)
            res = _dot(f_ref[...], t)
            pick = pl.ds(j, n2, stride=_V7X_SUBLANES)
            x2[q, pick, :] = _pack_pair(res[:n2], res[n2:])
        x_ref[0, :, :, ls] = x2[q].reshape(slab)


def _dft_stage2(t, f2):
    b, n1, n2, c = t.shape
    inb = pl.BlockSpec((1, _V7X_SUBLANES, n2, c), lambda i, j: (i, j, 0, 0))
    outb = pl.BlockSpec((1, n2, _V7X_SUBLANES, c), lambda i, j: (i, 0, j, 0))
    rows = (c // _V7X_LANES, n2 * _V7X_SUBLANES, _V7X_LANES)
    xp = pl.pallas_call(
        _dft_stage2_kernel,
        grid=(b, n1 // _V7X_SUBLANES),
        in_specs=[inb, _resident((2 * n2, 2 * n2))],
        out_specs=outb,
        out_shape=jax.ShapeDtypeStruct((b, n2, n1, c), jnp.uint32),
        scratch_shapes=[pltpu.VMEM(rows, jnp.uint32)],
        compiler_params=_params(2),
        name="dft_stage2",
    )(t, f2)
    return xp.reshape(b, n2 * n1, c)


def _ret_bwd_kernel(dec_ref, k_ref, v_ref, rb_ref, zb_ref, gc_ref, st_ref, *, cc, nch):
    b = pl.program_id(0)
    t = pl.program_id(1)

    @pl.when((b == 0) & (t == 0))
    def _():
        rowi = lax.broadcasted_iota(jnp.int32, (cc, RET_DV), 0).astype(F32)
        for h in range(RET_HEADS):
            lgb = _log_sigmoid(dec_ref[1, h])
            zb_ref[h] = jnp.exp(lgb * rowi)[:, :RET_DK]
            gc_ref[h] = jnp.exp(lgb * float(cc))

    @pl.when(t == 0)
    def _():
        st_ref[...] = jnp.zeros_like(st_ref)

    for h in range(RET_HEADS):
        st = st_ref[h]
        for c in reversed(range(nch)):
            rs = slice(c * cc, (c + 1) * cc)
            rb_ref[0, c, h * RET_DK:(h + 1) * RET_DK, :] = st.astype(BF16)
            kh = k_ref[0, rs, h * RET_DK:(h + 1) * RET_DK]
            vh = v_ref[0, rs, h * RET_DV:(h + 1) * RET_DV]
            kz = (kh.astype(F32) * zb_ref[h]).astype(BF16)
            st = gc_ref[h] * st + _dot_tn(kz, vh)
        st_ref[h] = st


def _ret_bwd(dec, k, v, *, cc, nch=4):
    b, s, _ = k.shape
    ns = s // (cc * nch)
    return pl.pallas_call(
        functools.partial(_ret_bwd_kernel, cc=cc, nch=nch),
        grid=(b, ns),
        in_specs=[
            _resident(dec.shape),
            pl.BlockSpec((1, cc * nch, RET_QK_WIDTH), lambda i, t: (i, ns - 1 - t, 0)),
            pl.BlockSpec((1, cc * nch, RET_V_WIDTH), lambda i, t: (i, ns - 1 - t, 0)),
        ],
        out_specs=pl.BlockSpec((1, nch, RET_QK_WIDTH, RET_DV), lambda i, t: (i, ns - 1 - t, 0, 0)),
        out_shape=jax.ShapeDtypeStruct((b, s // cc, RET_QK_WIDTH, RET_DV), BF16),
        scratch_shapes=[
            pltpu.VMEM((RET_HEADS, cc, RET_DK), F32),
            pltpu.VMEM((RET_HEADS, 1, RET_DV), F32),
            pltpu.VMEM((RET_HEADS, RET_DK, RET_DV), F32),
        ],
        compiler_params=_params(2),
        name="ret_bwd",
    )(dec, k, v)


def _ret_fwd_kernel(dec_ref, q_ref, k_ref, v_ref, sg_ref, rb_ref, o_ref,
                    dm_ref, xif_ref, xib_ref, zf_ref, gc_ref, st_ref, *, cc, nch):
    b = pl.program_id(0)
    j = pl.program_id(1)

    @pl.when((b == 0) & (j == 0))
    def _():
        row = lax.broadcasted_iota(jnp.int32, (cc, cc), 0)
        col = lax.broadcasted_iota(jnp.int32, (cc, cc), 1)
        diff = (row - col).astype(F32)
        rowi = lax.broadcasted_iota(jnp.int32, (cc, RET_DV), 0).astype(F32)
        for h in range(RET_HEADS):
            lgf = _log_sigmoid(dec_ref[0, h])
            lgb = _log_sigmoid(dec_ref[1, h])
            dm_ref[h] = jnp.where(diff >= 0.0,
                                  jnp.exp(lgf * jnp.maximum(diff, 0.0)),
                                  jnp.exp(lgb * jnp.maximum(-diff, 0.0)))
            xif_ref[h] = jnp.exp(lgf * (rowi + 1.0))[:, :RET_DK]
            xib_ref[h] = jnp.exp(lgb * (float(cc) - rowi))[:, :RET_DK]
            zf_ref[h] = jnp.exp(lgf * (float(cc) - 1.0 - rowi))[:, :RET_DK]
            gc_ref[h] = jnp.exp(lgf * float(cc))

    @pl.when(j == 0)
    def _():
        st_ref[...] = jnp.zeros_like(st_ref)

    for h in range(RET_HEADS):
        st = st_ref[h]
        for c in range(nch):
            rs = slice(c * cc, (c + 1) * cc)
            qh = q_ref[0, rs, h * RET_DK:(h + 1) * RET_DK]
            kh = k_ref[0, rs, h * RET_DK:(h + 1) * RET_DK]
            vh = v_ref[0, rs, h * RET_DV:(h + 1) * RET_DV]
            p = (_dot_nt(qh, kh) * dm_ref[h]).astype(BF16)
            qf32 = qh.astype(F32)
            qf = (qf32 * xif_ref[h]).astype(BF16)
            qb = (qf32 * xib_ref[h]).astype(BF16)
            lhs = jnp.concatenate([p, qf, qb], axis=1)
            rhs = jnp.concatenate(
                [vh, st.astype(BF16), rb_ref[0, c, h * RET_DK:(h + 1) * RET_DK, :]], axis=0)
            y = _dot(lhs, rhs)
            kz = (kh.astype(F32) * zf_ref[h]).astype(BF16)
            st = gc_ref[h] * st + _dot_tn(kz, vh)
            ms = jnp.mean(y * y, axis=-1, keepdims=True)
            yn = y * lax.rsqrt(ms + EPS)
            sg = sg_ref[0, rs, h * RET_DV:(h + 1) * RET_DV].astype(F32)
            o_ref[0, rs, h * RET_DV:(h + 1) * RET_DV] = (sg * yn).astype(BF16)
        st_ref[h] = st


def _ret_fwd(dec, q, k, v, sg, rb, *, cc, nch=4):
    b, s, _ = q.shape
    ns = s // (cc * nch)

    def tok(width):
        return pl.BlockSpec((1, cc * nch, width), lambda i, j: (i, j, 0))

    return pl.pallas_call(
        functools.partial(_ret_fwd_kernel, cc=cc, nch=nch),
        grid=(b, ns),
        in_specs=[
            _resident(dec.shape),
            tok(RET_QK_WIDTH), tok(RET_QK_WIDTH), tok(RET_V_WIDTH), tok(RET_V_WIDTH),
            pl.BlockSpec((1, nch, RET_QK_WIDTH, RET_DV), lambda i, j: (i, j, 0, 0)),
        ],
        out_specs=tok(RET_V_WIDTH),
        out_shape=jax.ShapeDtypeStruct((b, s, RET_V_WIDTH), BF16),
        scratch_shapes=[
            pltpu.VMEM((RET_HEADS, cc, cc), F32),
            pltpu.VMEM((RET_HEADS, cc, RET_DK), F32),
            pltpu.VMEM((RET_HEADS, cc, RET_DK), F32),
            pltpu.VMEM((RET_HEADS, cc, RET_DK), F32),
            pltpu.VMEM((RET_HEADS, 1, RET_DV), F32),
            pltpu.VMEM((RET_HEADS, RET_DK, RET_DV), F32),
        ],
        compiler_params=_params(2),
        name="ret_fwd",
    )(dec, q, k, v, sg, rb)


def _mix_out_kernel(x_ref, *rest, scale, rows):
    spectrum, (yr_ref, gt_ref, cs_ref, fw_ref, rw_ref, mw_ref, o_ref) = rest[:-7], rest[-7:]
    d = x_ref.shape[-1]
    for sl in _row_slices(x_ref.shape[0], rows):
        groups = []
        for ls in _lane_blocks(spectrum[0].shape[-1]):
            if len(spectrum) == 1:
                pair = _unpack_pair(spectrum[0][sl, ls])
            else:
                pair = (spectrum[0][sl, ls], spectrum[1][sl, ls])
            groups.append(_dot(jnp.concatenate(pair, axis=1).astype(BF16), cs_ref[...]))
        yf = jnp.concatenate(groups, axis=1) * scale
        ya = _dot(yf.astype(BF16), fw_ref[...])
        yb = _dot(yr_ref[sl, :], rw_ref[...])
        ga = gt_ref[sl, :d].astype(F32)
        gb = gt_ref[sl, d:].astype(F32)
        merged = (ga * ya + gb * yb).astype(BF16)
        o_ref[sl, :] = x_ref[sl, :] + _dot(merged, mw_ref[...])


def _mix_out(x2, spectrum, yr, gt, cs, fw, rw, mw, *, scale, tm=1024, rows=512):
    t, d = x2.shape

    def tok(width):
        return pl.BlockSpec((tm, width), lambda i: (i, 0))

    return pl.pallas_call(
        functools.partial(_mix_out_kernel, scale=scale, rows=rows),
        grid=(t // tm,),
        in_specs=[tok(d)] + [tok(FOURIER_WIDTH)] * len(spectrum) + [
            tok(RET_V_WIDTH), tok(2 * d),
            _resident(cs.shape), _resident(fw.shape), _resident(rw.shape), _resident(mw.shape)],
        out_specs=tok(d),
        out_shape=jax.ShapeDtypeStruct((t, d), F32),
        compiler_params=_params(1),
        name="mix_out",
    )(x2, *spectrum, yr, gt, cs, fw, rw, mw)


def _mem_kv_kernel(m_ref, g_ref, w_ref, o_ref):
    mn = _rmsnorm(m_ref[...], g_ref[...]).astype(BF16)
    o_ref[...] = _dot(mn, w_ref[...]).astype(BF16)


def _mem_kv(mem2, gain, w, *, tm=256):
    t, d = mem2.shape
    n = w.shape[1]
    return pl.pallas_call(
        _mem_kv_kernel,
        grid=(t // tm,),
        in_specs=[pl.BlockSpec((tm, d), lambda i: (i, 0)), _resident((1, d)), _resident((d, n))],
        out_specs=pl.BlockSpec((tm, n), lambda i: (i, 0)),
        out_shape=jax.ShapeDtypeStruct((t, n), BF16),
        compiler_params=_params(1),
        name="mem_kv",
    )(mem2, gain, w)


def _xattn_kernel(x_ref, g_ref, wq_ref, kv_ref, wo_ref, *rest, rows):
    n_cast = (len(rest) - 1) // 2
    o_ref = rest[n_cast]
    _cast_chunks(rest[:n_cast], rest[n_cast + 1:])
    d = x_ref.shape[-1]
    heads = [(h * XA_HEAD_DIM, (h + 1) * XA_HEAD_DIM) for h in range(XA_HEADS)]
    sls = _row_slices(x_ref.shape[1], rows)
    xs = [x_ref[0, sl, :] for sl in sls]
    qs = [_dot(_rmsnorm(x, g_ref[...]).astype(BF16), wq_ref[...]).astype(BF16) for x in xs]
    scores = [[_dot_nt(q[:, lo:hi], kv_ref[0, :, lo:hi]) * (XA_HEAD_DIM ** -0.5) for lo, hi in heads]
              for q in qs]
    probs = []
    for per_head in scores:
        row = []
        for s in per_head:
            e = jnp.exp(s - jnp.max(s, axis=-1, keepdims=True))
            row.append((e / jnp.sum(e, axis=-1, keepdims=True)).astype(BF16))
        probs.append(row)
    outs = [[_dot(p, kv_ref[0, :, d + lo:d + hi]).astype(BF16) for p, (lo, hi) in zip(row, heads)]
            for row in probs]
    for sl, x, per_head in zip(sls, xs, outs):
        o_ref[0, sl, :] = x + _dot(jnp.concatenate(per_head, axis=-1), wo_ref[...])


def _xattn(x3, gain, wq, kv, wo, *, cast=(), cast_layer=0, tm=1024, rows=512):
    b, s, d = x3.shape
    m = kv.shape[1]
    per_seq = s // tm
    tok = pl.BlockSpec((1, tm, d), lambda i, j: (i, j, 0))
    c_in, c_out, c_shapes = _cast_specs(cast, cast_layer, b * per_seq, lambda i, j: i * per_seq + j)
    outs = pl.pallas_call(
        functools.partial(_xattn_kernel, rows=rows),
        grid=(b, per_seq),
        in_specs=[tok, _resident((1, d)), _resident((d, d)),
                  pl.BlockSpec((1, m, 2 * d), lambda i, j: (i, 0, 0)),
                  _resident((d, d))] + c_in,
        out_specs=[tok] + c_out,
        out_shape=[jax.ShapeDtypeStruct((b, s, d), F32)] + c_shapes,
        compiler_params=_params(2),
        name="xattn",
    )(x3, gain, wq, kv, wo, *cast)
    return outs[0], list(outs[1:])


def _angle(num, n):
    return (2.0 * math.pi / n) * (num % n).astype(F32)


_DFT_TABLE_SPLIT = 32
_DFT_TABLE_DIRECT_MAX = 1024


def _dft_cos_sin(n):
    i = jnp.arange(n, dtype=jnp.int32)
    if n <= _DFT_TABLE_DIRECT_MAX:
        ang = _angle(i[:, None] * i[None, :], n)
        return jnp.cos(ang), jnp.sin(ang)
    na = n // _DFT_TABLE_SPLIT
    ang_a = _angle(i[:, None] * jnp.arange(na, dtype=jnp.int32)[None, :], na)[:, :, None]
    ang_b = _angle(i[:, None] * jnp.arange(_DFT_TABLE_SPLIT, dtype=jnp.int32)[None, :], n)[:, None, :]
    ca, sa, cb, sb = lax.optimization_barrier(
        (jnp.cos(ang_a), jnp.sin(ang_a), jnp.cos(ang_b), jnp.sin(ang_b)))
    return (ca * cb - sa * sb).reshape(n, n), (sa * cb + ca * sb).reshape(n, n)


def _rotary_tables(s):
    half = RET_DK // 2
    inv = 1.0 / (ROPE_BASE ** (jnp.arange(half, dtype=F32) * 2.0 / RET_DK))
    ang = jnp.arange(s, dtype=F32)[:, None] * inv[None, :]
    cos, sin = jnp.cos(ang), jnp.sin(ang)
    return jnp.concatenate([cos, cos], -1), jnp.concatenate([-sin, sin], -1)


def _seq_tables(s):
    tabs = {"rot": _rotary_tables(s)}
    if s >= _DFT_TWO_STAGE_MIN:
        n1, n2 = _DFT_N1, s // _DFT_N1
        c1, s1 = _dft_cos_sin(n1)
        tabs["f1"] = jnp.concatenate([c1, -s1], 0).astype(BF16)
        c2, s2 = _dft_cos_sin(n2)
        tabs["f2"] = jnp.concatenate(
            [jnp.concatenate([c2, s2], 1), jnp.concatenate([-s2, c2], 1)], 0).astype(BF16)
        ang = _angle(jnp.arange(n2, dtype=jnp.int32)[:, None] * jnp.arange(n1, dtype=jnp.int32)[None, :], s)
        lane = (n2, n1, _V7X_LANES)
        tabs["tw"] = (jnp.broadcast_to(jnp.cos(ang)[:, :, None], lane),
                      jnp.broadcast_to(jnp.sin(ang)[:, :, None], lane))
    else:
        c, sn = _dft_cos_sin(s)
        tabs["f"] = jnp.concatenate([c, -sn], 0).astype(BF16)
    return tabs


def _channel_dft_matrix():
    c, s = _dft_cos_sin(FOURIER_GROUP_DIM)
    return jnp.concatenate([c, s], 0).astype(BF16)


_CAST_BY_FFN1 = ("ffn1_w_in", "ffn1_w_out")
_CAST_BY_XATTN = ("mix_w_in", "fourier_w", "ret_w_out", "mix_w_out", "xa_wq", "xa_wkv", "xa_wo")
_CAST_BY_FFN2 = ("ffn2_w_in", "ffn2_w_out")


def _trunk(x, mem, small, wb, bd, stacked=None):
    b, s, d = x.shape
    m = mem.shape[1]
    tabs = _seq_tables(s)
    cosf, sinf = tabs["rot"]
    two_stage = "f" not in tabs
    scale = float((s * FOURIER_GROUP_DIM) ** -0.5)
    x2 = x.reshape(b * s, d)
    mem2 = mem.reshape(b * m, d)
    for l in range(DEPTH):
        sm, lw = small[l], wb[l]
        ahead = stacked is not None and l + 1 < DEPTH

        def cast_args(names):
            if not ahead:
                return {}
            return {"cast": [stacked[n] for n in names], "cast_layer": l + 1}

        def keep(names, copies):
            if ahead:
                wb.setdefault(l + 1, {}).update(zip(names, copies))

        x2, copies = _ffn(x2, sm["ffn1_norm"], lw["ffn1_w_in"], lw["ffn1_w_out"], small["final_norm"],
                          final_norm=False, **cast_args(_CAST_BY_FFN1))
        keep(_CAST_BY_FFN1, copies)
        hf, q, k, v, sg, gt = _mix_in(x2.reshape(b, s, d), sm["mix_norm"], lw["mix_w_in"], cosf, sinf,
                                      hf_dtype=F32 if two_stage else BF16)
        if two_stage:
            spectrum = (_dft_stage2(_dft_stage1(hf, tabs["f1"], *tabs["tw"], n1=_DFT_N1), tabs["f2"]),)
        else:
            spectrum = _dft_direct(hf, tabs["f"])
        rb = _ret_bwd(sm["dec"], k, v, cc=_RET_CHUNK)
        yr = _ret_fwd(sm["dec"], q, k, v, sg, rb, cc=_RET_CHUNK)
        x2 = _mix_out(x2, [a.reshape(b * s, -1) for a in spectrum], yr.reshape(b * s, -1),
                      gt.reshape(b * s, -1), bd, lw["fourier_w"], lw["ret_w_out"],
                      lw["mix_w_out"], scale=scale)
        kv = _mem_kv(mem2, sm["mem_norm"], lw["xa_wkv"]).reshape(b, m, 2 * d)
        x3, copies = _xattn(x2.reshape(b, s, d), sm["xa_norm"], lw["xa_wq"], kv, lw["xa_wo"],
                            **cast_args(_CAST_BY_XATTN))
        keep(_CAST_BY_XATTN, copies)
        x2, copies = _ffn(x3.reshape(b * s, d), sm["ffn2_norm"], lw["ffn2_w_in"], lw["ffn2_w_out"],
                          small["final_norm"], final_norm=(l == DEPTH - 1), **cast_args(_CAST_BY_FFN2))
        keep(_CAST_BY_FFN2, copies)
    return x2.reshape(b, s, d)


def kernel(x_prompt, x_sample, mem_prompt, mem_sample, ffn1_norm, ffn1_w_in, ffn1_w_out, mix_norm, mix_w_in, fourier_w, ret_decay_fwd, ret_decay_bwd, ret_w_out, mix_w_out, xa_norm, mem_norm, xa_wq, xa_wkv, xa_wo, ffn2_norm, ffn2_w_in, ffn2_w_out, final_norm):
    row = lambda a: a.reshape(1, -1)
    small = {"final_norm": row(final_norm)}
    for l in range(DEPTH):
        dec = jnp.stack([ret_decay_fwd[l], ret_decay_bwd[l]])
        small[l] = {
            "ffn1_norm": row(ffn1_norm[l]), "mix_norm": row(mix_norm[l]), "xa_norm": row(xa_norm[l]),
            "mem_norm": row(mem_norm[l]), "ffn2_norm": row(ffn2_norm[l]),
            "dec": jnp.broadcast_to(dec[:, :, None, None], (2, RET_HEADS, 1, RET_DV)),
        }
    stacked = {
        "ffn1_w_in": ffn1_w_in, "ffn1_w_out": ffn1_w_out, "mix_w_in": mix_w_in, "fourier_w": fourier_w,
        "ret_w_out": ret_w_out, "mix_w_out": mix_w_out, "xa_wq": xa_wq, "xa_wkv": xa_wkv, "xa_wo": xa_wo,
        "ffn2_w_in": ffn2_w_in, "ffn2_w_out": ffn2_w_out,
    }
    wb = {0: {name: a[0].astype(BF16) for name, a in stacked.items()}}
    bd = _channel_dft_matrix()
    y_prompt = _trunk(x_prompt, mem_prompt, small, wb, bd, stacked)
    y_sample = _trunk(x_sample, mem_sample, small, wb, bd)
    return (y_prompt, y_sample)
```

```python
import functools
import math

import jax
import jax.numpy as jnp
from jax import lax
from jax.experimental import pallas as pl
from jax.experimental.pallas import tpu as pltpu

F32 = jnp.float32
BF16 = jnp.bfloat16

D_MODEL = 1024
DEPTH = 4
D_FF = 2816
FOURIER_GROUP_DIM = 128
FOURIER_WIDTH = 512
RET_HEADS = 4
RET_DK = 128
RET_DV = 256
RET_QK_WIDTH = RET_HEADS * RET_DK
RET_V_WIDTH = RET_HEADS * RET_DV
XA_HEADS = 4
XA_HEAD_DIM = D_MODEL // XA_HEADS
ROPE_BASE = 10000.0
EPS = 1e-6

_C_F = (0, FOURIER_WIDTH)
_C_Q = (_C_F[1], _C_F[1] + RET_QK_WIDTH)
_C_K = (_C_Q[1], _C_Q[1] + RET_QK_WIDTH)
_C_V = (_C_K[1], _C_K[1] + RET_V_WIDTH)
_C_G = (_C_V[1], _C_V[1] + RET_V_WIDTH)
_C_GATES = (_C_G[1], _C_G[1] + 2 * D_MODEL)

_V7X_VMEM_BYTES = 64 * 1024 * 1024
_VMEM_LIMIT = _V7X_VMEM_BYTES - 8 * 1024 * 1024
_V7X_LANES = 128
_V7X_SUBLANES = 8

_RET_CHUNK = 256
_DFT_TWO_STAGE_MIN = 16384
_DFT_N1 = 128


def _params(n_axes):
    return pltpu.CompilerParams(
        dimension_semantics=("arbitrary",) * n_axes, vmem_limit_bytes=_VMEM_LIMIT)


def _resident(shape):
    nd = len(shape)
    return pl.BlockSpec(shape, lambda *_: (0,) * nd, pipeline_mode=pl.Buffered(1))


def _rmsnorm(x, g):
    ms = jnp.mean(x * x, axis=-1, keepdims=True)
    return x * lax.rsqrt(ms + EPS) * g


def _sigmoid(x):
    return 1.0 / (1.0 + jnp.exp(-x))


def _log_sigmoid(x):
    return -(jnp.maximum(-x, 0.0) + jnp.log1p(jnp.exp(-jnp.abs(x))))


def _dot(a, b):
    return jnp.dot(a, b, preferred_element_type=F32)


def _dot_nt(a, b):
    return lax.dot_general(a, b, (((1,), (1,)), ((), ())), preferred_element_type=F32)


def _dot_tn(a, b):
    return lax.dot_general(a, b, (((0,), (0,)), ((), ())), preferred_element_type=F32)


def _row_slices(total, rows):
    return [slice(r * rows, (r + 1) * rows) for r in range(total // rows)]


def _cast_specs(stacked, layer, nsteps, step_of):
    in_specs, out_specs, out_shapes = [], [], []
    for a in stacked:
        _, nrows, ncols = a.shape
        rc = nrows // nsteps
        assert rc * nsteps == nrows and rc % (2 * _V7X_SUBLANES) == 0, (a.shape, nsteps)
        in_specs.append(pl.BlockSpec((None, rc, ncols), lambda *g: (layer, step_of(*g), 0)))
        out_specs.append(pl.BlockSpec((rc, ncols), lambda *g: (step_of(*g), 0)))
        out_shapes.append(jax.ShapeDtypeStruct((nrows, ncols), BF16))
    return in_specs, out_specs, out_shapes


def _cast_chunks(src_refs, dst_refs):
    for src, dst in zip(src_refs, dst_refs):
        dst[...] = src[...].astype(BF16)


def _ffn_kernel(x_ref, g_ref, wg_ref, wu_ref, wo_ref, fg_ref, *rest, rows, tf, final_norm):
    n_cast = (len(rest) - 1) // 2
    o_ref = rest[n_cast]
    _cast_chunks(rest[:n_cast], rest[n_cast + 1:])
    f = wg_ref.shape[1]
    for sl in _row_slices(x_ref.shape[0], rows):
        x = x_ref[sl, :]
        xn = _rmsnorm(x, g_ref[...]).astype(BF16)
        acc = None
        for c in range(f // tf):
            cs = slice(c * tf, (c + 1) * tf)
            gate = _dot(xn, wg_ref[:, cs])
            up = _dot(xn, wu_ref[:, cs])
            act = (gate * _sigmoid(gate) * up).astype(BF16)
            part = _dot(act, wo_ref[cs, :])
            acc = part if acc is None else acc + part
        y = x + 0.5 * acc
        if final_norm:
            y = _rmsnorm(y, fg_ref[...])
        o_ref[sl, :] = y


def _ffn(x2, gain, w_in, w_out, final_gain, *, final_norm, cast=(), cast_layer=0,
         tm=1024, rows=256, tf=1408):
    t, d = x2.shape
    f = w_out.shape[0]
    nsteps = t // tm
    c_in, c_out, c_shapes = _cast_specs(cast, cast_layer, nsteps, lambda i: i)
    outs = pl.pallas_call(
        functools.partial(_ffn_kernel, rows=rows, tf=tf, final_norm=final_norm),
        grid=(nsteps,),
        in_specs=[
            pl.BlockSpec((tm, d), lambda i: (i, 0)),
            _resident((1, d)),
            pl.BlockSpec((d, f), lambda i: (0, 0), pipeline_mode=pl.Buffered(1)),
            pl.BlockSpec((d, f), lambda i: (0, 1), pipeline_mode=pl.Buffered(1)),
            _resident((f, d)), _resident((1, d)),
        ] + c_in,
        out_specs=[pl.BlockSpec((tm, d), lambda i: (i, 0))] + c_out,
        out_shape=[jax.ShapeDtypeStruct((t, d), F32)] + c_shapes,
        compiler_params=_params(1),
        name="ffn",
    )(x2, gain, w_in, w_in, w_out, final_gain, *cast)
    return outs[0], list(outs[1:])


def _mix_in_kernel(x_ref, g_ref, w_ref, cos_ref, sin_ref,
                   hf_ref, q_ref, k_ref, v_ref, sg_ref, gt_ref, *, rows):
    def rotary(hx, out_ref, sl, scale):
        cosf = cos_ref[sl, :]
        sinf = sin_ref[sl, :]
        for h in range(RET_HEADS):
            blk = hx[:, h * RET_DK:(h + 1) * RET_DK]
            rot = blk * cosf + pltpu.roll(blk, RET_DK // 2, 1) * sinf
            if scale != 1.0:
                rot = rot * scale
            out_ref[0, sl, h * RET_DK:(h + 1) * RET_DK] = rot.astype(out_ref.dtype)

    for sl in _row_slices(x_ref.shape[1], rows):
        xn = _rmsnorm(x_ref[0, sl, :], g_ref[...]).astype(BF16)

        def proj(cols):
            return _dot(xn, w_ref[:, cols[0]:cols[1]])

        hf_ref[0, sl, :] = proj(_C_F).astype(hf_ref.dtype)
        rotary(proj(_C_Q), q_ref, sl, 1.0)
        rotary(proj(_C_K), k_ref, sl, RET_DK ** -0.5)
        v_ref[0, sl, :] = proj(_C_V).astype(v_ref.dtype)
        hg = proj(_C_G)
        sg_ref[0, sl, :] = (hg * _sigmoid(hg)).astype(sg_ref.dtype)
        gt_ref[0, sl, :] = _sigmoid(proj(_C_GATES)).astype(gt_ref.dtype)


def _mix_in(x3, gain, w, cosf, sinf, *, hf_dtype, tm=1024, rows=256):
    b, s, d = x3.shape
    n = w.shape[1]

    def tok(width):
        return pl.BlockSpec((1, tm, width), lambda i, j: (i, j, 0))

    def out(width, dtype=BF16):
        return jax.ShapeDtypeStruct((b, s, width), dtype)

    return pl.pallas_call(
        functools.partial(_mix_in_kernel, rows=rows),
        grid=(b, s // tm),
        in_specs=[
            tok(d), _resident((1, d)), _resident((d, n)),
            pl.BlockSpec((tm, RET_DK), lambda i, j: (j, 0)),
            pl.BlockSpec((tm, RET_DK), lambda i, j: (j, 0)),
        ],
        out_specs=[tok(FOURIER_WIDTH), tok(RET_QK_WIDTH), tok(RET_QK_WIDTH),
                   tok(RET_V_WIDTH), tok(RET_V_WIDTH), tok(2 * D_MODEL)],
        out_shape=[out(FOURIER_WIDTH, hf_dtype), out(RET_QK_WIDTH), out(RET_QK_WIDTH),
                   out(RET_V_WIDTH), out(RET_V_WIDTH), out(2 * D_MODEL)],
        compiler_params=_params(2),
        name="mix_in",
    )(x3, gain, w, cosf, sinf)


def _dft_direct_kernel(a_ref, f_ref, xr_ref, xi_ref):
    s = a_ref.shape[1]
    res = _dot(f_ref[...], a_ref[0])
    xr_ref[0] = res[:s].astype(BF16)
    xi_ref[0] = res[s:].astype(BF16)


def _dft_direct(hf, fmat, *, bw=512):
    b, s, c = hf.shape
    spec = pl.BlockSpec((1, s, bw), lambda i, j: (i, 0, j))
    return pl.pallas_call(
        _dft_direct_kernel,
        grid=(b, c // bw),
        in_specs=[spec, _resident((2 * s, s))],
        out_specs=[spec, spec],
        out_shape=[jax.ShapeDtypeStruct((b, s, c), BF16)] * 2,
        compiler_params=_params(2),
        name="dft_direct",
    )(hf, fmat)


def _lane_blocks(c):
    return [slice(q * _V7X_LANES, (q + 1) * _V7X_LANES) for q in range(c // _V7X_LANES)]


def _pack_pair(re, im):
    hi = lax.bitcast_convert_type(re.astype(BF16).astype(F32), jnp.uint32)
    lo = lax.bitcast_convert_type(im.astype(BF16).astype(F32), jnp.uint32)
    return hi | (lo >> 16)


def _unpack_pair(w):
    re = lax.bitcast_convert_type(w & jnp.uint32(0xFFFF0000), F32)
    im = lax.bitcast_convert_type(w << 16, F32)
    return re, im


def _dft_stage1_kernel(a_ref, f_ref, tc_ref, ts_ref, t_ref, a2, t2):
    n1 = a_ref.shape[1]
    slab = (n1, _V7X_SUBLANES, _V7X_LANES)
    for q, ls in enumerate(_lane_blocks(a_ref.shape[-1])):
        a2[q] = a_ref[0, :, :, ls].reshape(a2.shape[1:])
        for j in range(_V7X_SUBLANES):
            pick = pl.ds(j, n1, stride=_V7X_SUBLANES)
            res = _dot(f_ref[...], a2[q, pick, :].astype(BF16))
            ur, ui = res[:n1], res[n1:]
            cw, sw = tc_ref[j], ts_ref[j]
            t2[q, pick, :] = _pack_pair(ur * cw + ui * sw, ui * cw - ur * sw)
        t_ref[0, :, :, ls] = t2[q].reshape(slab)


def _dft_stage1(hf, f1, twc, tws, *, n1):
    b, s, c = hf.shape
    n2 = s // n1
    blk = pl.BlockSpec((1, n1, _V7X_SUBLANES, c), lambda i, j: (i, 0, j, 0))
    tw = pl.BlockSpec((_V7X_SUBLANES, n1, _V7X_LANES), lambda i, j: (j, 0, 0))
    rows = (c // _V7X_LANES, n1 * _V7X_SUBLANES, _V7X_LANES)
    return pl.pallas_call(
        _dft_stage1_kernel,
        grid=(b, n2 // _V7X_SUBLANES),
        in_specs=[blk, _resident((2 * n1, n1)), tw, tw],
        out_specs=blk,
        out_shape=jax.ShapeDtypeStruct((b, n1, n2, c), jnp.uint32),
        scratch_shapes=[pltpu.VMEM(rows, F32), pltpu.VMEM(rows, jnp.uint32)],
        compiler_params=_params(2),
        name="dft_stage1",
    )(hf.reshape(b, n1, n2, c), f1, twc, tws)


def _dft_stage2_kernel(t_ref, f_ref, x_ref, x2):
    n2 = t_ref.shape[2]
    slab = (n2, _V7X_SUBLANES, _V7X_LANES)
    for q, ls in enumerate(_lane_blocks(t_ref.shape[-1])):
        for j in range(_V7X_SUBLANES):
            t = jnp.concatenate(_unpack_pair(t_ref[0, j, :, ls]), axis=0).astype(BF16)
            res = _dot(f_ref[...], t)
            pick = pl.ds(j, n2, stride=_V7X_SUBLANES)
            x2[q, pick, :] = _pack_pair(res[:n2], res[n2:])
        x_ref[0, :, :, ls] = x2[q].reshape(slab)


def _dft_stage2(t, f2):
    b, n1, n2, c = t.shape
    inb = pl.BlockSpec((1, _V7X_SUBLANES, n2, c), lambda i, j: (i, j, 0, 0))
    outb = pl.BlockSpec((1, n2, _V7X_SUBLANES, c), lambda i, j: (i, 0, j, 0))
    rows = (c // _V7X_LANES, n2 * _V7X_SUBLANES, _V7X_LANES)
    xp = pl.pallas_call(
        _dft_stage2_kernel,
        grid=(b, n1 // _V7X_SUBLANES),
        in_specs=[inb, _resident((2 * n2, 2 * n2))],
        out_specs=outb,
        out_shape=jax.ShapeDtypeStruct((b, n2, n1, c), jnp.uint32),
        scratch_shapes=[pltpu.VMEM(rows, jnp.uint32)],
        compiler_params=_params(2),
        name="dft_stage2",
    )(t, f2)
    return xp.reshape(b, n2 * n1, c)


def _retention_kernel(dec_ref, q_ref, k_ref, v_ref, sg_ref, o_ref,
                      dm_ref, xif_ref, xib_ref, zf_ref, zb_ref, gcf_ref, gcb_ref, st_ref, rb_ref,
                      *, cc, nch, ns):
    b = pl.program_id(0)
    t = pl.program_id(1)

    @pl.when((b == 0) & (t == 0))
    def _():
        row = lax.broadcasted_iota(jnp.int32, (cc, cc), 0)
        col = lax.broadcasted_iota(jnp.int32, (cc, cc), 1)
        diff = (row - col).astype(F32)
        rowi = lax.broadcasted_iota(jnp.int32, (cc, RET_DV), 0).astype(F32)
        for h in range(RET_HEADS):
            lgf = _log_sigmoid(dec_ref[0, h])
            lgb = _log_sigmoid(dec_ref[1, h])
            dm_ref[h] = jnp.where(diff >= 0.0,
                                  jnp.exp(lgf * jnp.maximum(diff, 0.0)),
                                  jnp.exp(lgb * jnp.maximum(-diff, 0.0)))
            xif_ref[h] = jnp.exp(lgf * (rowi + 1.0))[:, :RET_DK]
            xib_ref[h] = jnp.exp(lgb * (float(cc) - rowi))[:, :RET_DK]
            zf_ref[h] = jnp.exp(lgf * (float(cc) - 1.0 - rowi))[:, :RET_DK]
            zb_ref[h] = jnp.exp(lgb * rowi)[:, :RET_DK]
            gcf_ref[h] = jnp.exp(lgf * float(cc))
            gcb_ref[h] = jnp.exp(lgb * float(cc))

    @pl.when((t == 0) | (t == ns))
    def _():
        st_ref[...] = jnp.zeros_like(st_ref)

    @pl.when(t < ns)
    def _():
        first = (ns - 1 - t) * nch
        for h in range(RET_HEADS):
            st = st_ref[h]
            for c in reversed(range(nch)):
                rs = slice(c * cc, (c + 1) * cc)
                rb_ref[first + c, h * RET_DK:(h + 1) * RET_DK, :] = st.astype(BF16)
                kh = k_ref[0, rs, h * RET_DK:(h + 1) * RET_DK]
                vh = v_ref[0, rs, h * RET_DV:(h + 1) * RET_DV]
                kz = (kh.astype(F32) * zb_ref[h]).astype(BF16)
                st = gcb_ref[h] * st + _dot_tn(kz, vh)
            st_ref[h] = st

    @pl.when(t >= ns)
    def _():
        first = (t - ns) * nch
        for h in range(RET_HEADS):
            st = st_ref[h]
            for c in range(nch):
                rs = slice(c * cc, (c + 1) * cc)
                qh = q_ref[0, rs, h * RET_DK:(h + 1) * RET_DK]
                kh = k_ref[0, rs, h * RET_DK:(h + 1) * RET_DK]
                vh = v_ref[0, rs, h * RET_DV:(h + 1) * RET_DV]
                p = (_dot_nt(qh, kh) * dm_ref[h]).astype(BF16)
                qf32 = qh.astype(F32)
                qf = (qf32 * xif_ref[h]).astype(BF16)
                qb = (qf32 * xib_ref[h]).astype(BF16)
                lhs = jnp.concatenate([p, qf, qb], axis=1)
                rhs = jnp.concatenate(
                    [vh, st.astype(BF16), rb_ref[first + c, h * RET_DK:(h + 1) * RET_DK, :]], axis=0)
                y = _dot(lhs, rhs)
                kz = (kh.astype(F32) * zf_ref[h]).astype(BF16)
                st = gcf_ref[h] * st + _dot_tn(kz, vh)
                ms = jnp.mean(y * y, axis=-1, keepdims=True)
                yn = y * lax.rsqrt(ms + EPS)
                sg = sg_ref[0, rs, h * RET_DV:(h + 1) * RET_DV].astype(F32)
                o_ref[0, rs, h * RET_DV:(h + 1) * RET_DV] = (sg * yn).astype(BF16)
            st_ref[h] = st


def _retention(dec, q, k, v, sg, *, cc, nch=4):
    b, s, _ = q.shape
    nch = min(nch, s // cc)
    ns = s // (cc * nch)

    def both(width):
        return pl.BlockSpec((1, cc * nch, width),
                            lambda i, t: (i, jnp.where(t < ns, ns - 1 - t, t - ns), 0))

    def fwd(width):
        return pl.BlockSpec((1, cc * nch, width), lambda i, t: (i, jnp.maximum(t - ns, 0), 0))

    vec = pltpu.VMEM((RET_HEADS, cc, RET_DK), F32)
    gain = pltpu.VMEM((RET_HEADS, 1, RET_DV), F32)
    return pl.pallas_call(
        functools.partial(_retention_kernel, cc=cc, nch=nch, ns=ns),
        grid=(b, 2 * ns),
        in_specs=[_resident(dec.shape), fwd(RET_QK_WIDTH), both(RET_QK_WIDTH), both(RET_V_WIDTH),
                  fwd(RET_V_WIDTH)],
        out_specs=fwd(RET_V_WIDTH),
        out_shape=jax.ShapeDtypeStruct((b, s, RET_V_WIDTH), BF16),
        scratch_shapes=[
            pltpu.VMEM((RET_HEADS, cc, cc), F32), vec, vec, vec, vec, gain, gain,
            pltpu.VMEM((RET_HEADS, RET_DK, RET_DV), F32),
            pltpu.VMEM((s // cc, RET_QK_WIDTH, RET_DV), BF16),
        ],
        compiler_params=_params(2),
        name="retention",
    )(dec, q, k, v, sg)


def _mix_attn_kernel(x_ref, *rest, scale, rows):
    spectrum = rest[:-11]
    (yr_ref, gt_ref, cs_ref, fw_ref, rw_ref, mw_ref, g_ref, wq_ref, kv_ref, wo_ref, o_ref) = rest[-11:]
    d = x_ref.shape[-1]
    heads = [(h * XA_HEAD_DIM, (h + 1) * XA_HEAD_DIM) for h in range(XA_HEADS)]
    sls = _row_slices(x_ref.shape[1], rows)
    xs = []
    for sl in sls:
        groups = []
        for ls in _lane_blocks(spectrum[0].shape[-1]):
            if len(spectrum) == 1:
                pair = _unpack_pair(spectrum[0][0, sl, ls])
            else:
                pair = (spectrum[0][0, sl, ls], spectrum[1][0, sl, ls])
            groups.append(_dot(jnp.concatenate(pair, axis=1).astype(BF16), cs_ref[...]))
        yf = jnp.concatenate(groups, axis=1) * scale
        ya = _dot(yf.astype(BF16), fw_ref[...])
        yb = _dot(yr_ref[0, sl, :], rw_ref[...])
        ga = gt_ref[0, sl, :d].astype(F32)
        gb = gt_ref[0, sl, d:].astype(F32)
        merged = (ga * ya + gb * yb).astype(BF16)
        xs.append(x_ref[0, sl, :] + _dot(merged, mw_ref[...]))
    qs = [_dot(_rmsnorm(x, g_ref[...]).astype(BF16), wq_ref[...]).astype(BF16) for x in xs]
    scores = [[_dot_nt(q[:, lo:hi], kv_ref[0, :, lo:hi]) * (XA_HEAD_DIM ** -0.5) for lo, hi in heads]
              for q in qs]
    probs = []
    for per_head in scores:
        row = []
        for s in per_head:
            e = jnp.exp(s - jnp.max(s, axis=-1, keepdims=True))
            row.append((e / jnp.sum(e, axis=-1, keepdims=True)).astype(BF16))
        probs.append(row)
    outs = [[_dot(p, kv_ref[0, :, d + lo:d + hi]).astype(BF16) for p, (lo, hi) in zip(row, heads)]
            for row in probs]
    for sl, x, per_head in zip(sls, xs, outs):
        o_ref[0, sl, :] = x + _dot(jnp.concatenate(per_head, axis=-1), wo_ref[...])


def _mix_attn(x3, spectrum, yr, gt, cs, fw, rw, mw, gain, wq, kv, wo, *, scale, tm=1024, rows=512):
    b, s, d = x3.shape
    m = kv.shape[1]

    def tok(width):
        return pl.BlockSpec((1, tm, width), lambda i, j: (i, j, 0))

    return pl.pallas_call(
        functools.partial(_mix_attn_kernel, scale=scale, rows=rows),
        grid=(b, s // tm),
        in_specs=[tok(d)] + [tok(FOURIER_WIDTH)] * len(spectrum) + [
            tok(RET_V_WIDTH), tok(2 * d),
            _resident(cs.shape), _resident(fw.shape), _resident(rw.shape), _resident(mw.shape),
            _resident((1, d)), _resident((d, d)),
            pl.BlockSpec((1, m, 2 * d), lambda i, j: (i, 0, 0)),
            _resident((d, d))],
        out_specs=tok(d),
        out_shape=jax.ShapeDtypeStruct((b, s, d), F32),
        compiler_params=_params(2),
        name="mix_attn",
    )(x3, *spectrum, yr, gt, cs, fw, rw, mw, gain, wq, kv, wo)


def _mem_kv_kernel(m_ref, g_ref, w_ref, o_ref):
    mn = _rmsnorm(m_ref[...], g_ref[...]).astype(BF16)
    o_ref[...] = _dot(mn, w_ref[...]).astype(BF16)


def _mem_kv(mem2, gain, w, *, tm=256):
    t, d = mem2.shape
    n = w.shape[1]
    return pl.pallas_call(
        _mem_kv_kernel,
        grid=(t // tm,),
        in_specs=[pl.BlockSpec((tm, d), lambda i: (i, 0)), _resident((1, d)), _resident((d, n))],
        out_specs=pl.BlockSpec((tm, n), lambda i: (i, 0)),
        out_shape=jax.ShapeDtypeStruct((t, n), BF16),
        compiler_params=_params(1),
        name="mem_kv",
    )(mem2, gain, w)


def _angle(num, n):
    return (2.0 * math.pi / n) * (num % n).astype(F32)


_DFT_TABLE_SPLIT = 32
_DFT_TABLE_DIRECT_MAX = 1024


def _dft_cos_sin(n):
    i = jnp.arange(n, dtype=jnp.int32)
    if n <= _DFT_TABLE_DIRECT_MAX:
        ang = _angle(i[:, None] * i[None, :], n)
        return jnp.cos(ang), jnp.sin(ang)
    na = n // _DFT_TABLE_SPLIT
    ang_a = _angle(i[:, None] * jnp.arange(na, dtype=jnp.int32)[None, :], na)[:, :, None]
    ang_b = _angle(i[:, None] * jnp.arange(_DFT_TABLE_SPLIT, dtype=jnp.int32)[None, :], n)[:, None, :]
    ca, sa, cb, sb = lax.optimization_barrier(
        (jnp.cos(ang_a), jnp.sin(ang_a), jnp.cos(ang_b), jnp.sin(ang_b)))
    return (ca * cb - sa * sb).reshape(n, n), (sa * cb + ca * sb).reshape(n, n)


def _rotary_tables(s):
    half = RET_DK // 2
    inv = 1.0 / (ROPE_BASE ** (jnp.arange(half, dtype=F32) * 2.0 / RET_DK))
    ang = jnp.arange(s, dtype=F32)[:, None] * inv[None, :]
    cos, sin = jnp.cos(ang), jnp.sin(ang)
    return jnp.concatenate([cos, cos], -1), jnp.concatenate([-sin, sin], -1)


def _seq_tables(s):
    tabs = {"rot": _rotary_tables(s)}
    if s >= _DFT_TWO_STAGE_MIN:
        n1, n2 = _DFT_N1, s // _DFT_N1
        c1, s1 = _dft_cos_sin(n1)
        tabs["f1"] = jnp.concatenate([c1, -s1], 0).astype(BF16)
        c2, s2 = _dft_cos_sin(n2)
        tabs["f2"] = jnp.concatenate(
            [jnp.concatenate([c2, s2], 1), jnp.concatenate([-s2, c2], 1)], 0).astype(BF16)
        ang = _angle(jnp.arange(n2, dtype=jnp.int32)[:, None] * jnp.arange(n1, dtype=jnp.int32)[None, :], s)
        lane = (n2, n1, _V7X_LANES)
        tabs["tw"] = (jnp.broadcast_to(jnp.cos(ang)[:, :, None], lane),
                      jnp.broadcast_to(jnp.sin(ang)[:, :, None], lane))
    else:
        c, sn = _dft_cos_sin(s)
        tabs["f"] = jnp.concatenate([c, -sn], 0).astype(BF16)
    return tabs


def _channel_dft_matrix():
    c, s = _dft_cos_sin(FOURIER_GROUP_DIM)
    return jnp.concatenate([c, s], 0).astype(BF16)


_CAST_BY_FFN1 = ("mix_w_in", "ffn2_w_in", "ffn2_w_out")
_CAST_BY_FFN2 = ("ffn1_w_in", "ffn1_w_out", "fourier_w", "ret_w_out", "mix_w_out",
                 "xa_wq", "xa_wkv", "xa_wo")


def _trunk(x, mem, small, wb, bd, stacked=None):
    b, s, d = x.shape
    m = mem.shape[1]
    tabs = _seq_tables(s)
    cosf, sinf = tabs["rot"]
    two_stage = "f" not in tabs
    scale = float((s * FOURIER_GROUP_DIM) ** -0.5)
    x2 = x.reshape(b * s, d)
    mem2 = mem.reshape(b * m, d)
    for l in range(DEPTH):
        sm, lw = small[l], wb.setdefault(l, {})

        def cast_args(names, layer):
            if stacked is None or layer >= DEPTH:
                return {}
            return {"cast": [stacked[n] for n in names], "cast_layer": layer}

        def keep(names, copies, layer):
            if copies:
                wb.setdefault(layer, {}).update(zip(names, copies))

        x2, copies = _ffn(x2, sm["ffn1_norm"], lw["ffn1_w_in"], lw["ffn1_w_out"], small["final_norm"],
                          final_norm=False, **cast_args(_CAST_BY_FFN1, l))
        keep(_CAST_BY_FFN1, copies, l)
        hf, q, k, v, sg, gt = _mix_in(x2.reshape(b, s, d), sm["mix_norm"], lw["mix_w_in"], cosf, sinf,
                                      hf_dtype=F32 if two_stage else BF16)
        if two_stage:
            spectrum = (_dft_stage2(_dft_stage1(hf, tabs["f1"], *tabs["tw"], n1=_DFT_N1), tabs["f2"]),)
        else:
            spectrum = _dft_direct(hf, tabs["f"])
        yr = _retention(sm["dec"], q, k, v, sg, cc=_RET_CHUNK)
        kv = _mem_kv(mem2, sm["mem_norm"], lw["xa_wkv"]).reshape(b, m, 2 * d)
        x3 = _mix_attn(x2.reshape(b, s, d), spectrum, yr, gt, bd, lw["fourier_w"], lw["ret_w_out"],
                       lw["mix_w_out"], sm["xa_norm"], lw["xa_wq"], kv, lw["xa_wo"], scale=scale)
        x2, copies = _ffn(x3.reshape(b * s, d), sm["ffn2_norm"], lw["ffn2_w_in"], lw["ffn2_w_out"],
                          small["final_norm"], final_norm=(l == DEPTH - 1),
                          **cast_args(_CAST_BY_FFN2, l + 1))
        keep(_CAST_BY_FFN2, copies, l + 1)
    return x2.reshape(b, s, d)


def kernel(x_prompt, x_sample, mem_prompt, mem_sample, ffn1_norm, ffn1_w_in, ffn1_w_out, mix_norm, mix_w_in, fourier_w, ret_decay_fwd, ret_decay_bwd, ret_w_out, mix_w_out, xa_norm, mem_norm, xa_wq, xa_wkv, xa_wo, ffn2_norm, ffn2_w_in, ffn2_w_out, final_norm):
    row = lambda a: a.reshape(1, -1)
    small = {"final_norm": row(final_norm)}
    for l in range(DEPTH):
        dec = jnp.stack([ret_decay_fwd[l], ret_decay_bwd[l]])
        small[l] = {
            "ffn1_norm": row(ffn1_norm[l]), "mix_norm": row(mix_norm[l]), "xa_norm": row(xa_norm[l]),
            "mem_norm": row(mem_norm[l]), "ffn2_norm": row(ffn2_norm[l]),
            "dec": jnp.broadcast_to(dec[:, :, None, None], (2, RET_HEADS, 1, RET_DV)),
        }
    stacked = {
        "ffn1_w_in": ffn1_w_in, "ffn1_w_out": ffn1_w_out, "mix_w_in": mix_w_in, "fourier_w": fourier_w,
        "ret_w_out": ret_w_out, "mix_w_out": mix_w_out, "xa_wq": xa_wq, "xa_wkv": xa_wkv, "xa_wo": xa_wo,
        "ffn2_w_in": ffn2_w_in, "ffn2_w_out": ffn2_w_out,
    }
    wb = {0: {name: stacked[name][0].astype(BF16) for name in _CAST_BY_FFN2}}
    bd = _channel_dft_matrix()
    y_prompt = _trunk(x_prompt, mem_prompt, small, wb, bd, stacked)
    y_sample = _trunk(x_sample, mem_sample, small, wb, bd)
    return (y_prompt, y_sample)
```

```python
import functools
import math

import jax
import jax.numpy as jnp
from jax import lax
from jax.experimental import pallas as pl
from jax.experimental.pallas import tpu as pltpu

F32 = jnp.float32
BF16 = jnp.bfloat16

D_MODEL = 1024
DEPTH = 4
D_FF = 2816
FOURIER_GROUP_DIM = 128
FOURIER_WIDTH = 512
RET_HEADS = 4
RET_DK = 128
RET_DV = 256
RET_QK_WIDTH = RET_HEADS * RET_DK
RET_V_WIDTH = RET_HEADS * RET_DV
XA_HEADS = 4
XA_HEAD_DIM = D_MODEL // XA_HEADS
ROPE_BASE = 10000.0
EPS = 1e-6

_C_F = (0, FOURIER_WIDTH)
_C_Q = (_C_F[1], _C_F[1] + RET_QK_WIDTH)
_C_K = (_C_Q[1], _C_Q[1] + RET_QK_WIDTH)
_C_V = (_C_K[1], _C_K[1] + RET_V_WIDTH)
_C_G = (_C_V[1], _C_V[1] + RET_V_WIDTH)
_C_GATES = (_C_G[1], _C_G[1] + 2 * D_MODEL)

_V7X_VMEM_BYTES = 64 * 1024 * 1024
_VMEM_LIMIT = _V7X_VMEM_BYTES - 8 * 1024 * 1024
_V7X_LANES = 128
_V7X_SUBLANES = 8

_RET_CHUNK = 256
_DFT_TWO_STAGE_MIN = 16384
_DFT_N1 = 128


def _params(n_axes):
    return pltpu.CompilerParams(
        dimension_semantics=("arbitrary",) * n_axes, vmem_limit_bytes=_VMEM_LIMIT)


def _resident(shape):
    nd = len(shape)
    return pl.BlockSpec(shape, lambda *_: (0,) * nd, pipeline_mode=pl.Buffered(1))


def _rmsnorm(x, g):
    ms = jnp.mean(x * x, axis=-1, keepdims=True)
    return x * lax.rsqrt(ms + EPS) * g


def _sigmoid(x):
    return 1.0 / (1.0 + jnp.exp(-x))


def _log_sigmoid(x):
    return -(jnp.maximum(-x, 0.0) + jnp.log1p(jnp.exp(-jnp.abs(x))))


def _dot(a, b):
    return jnp.dot(a, b, preferred_element_type=F32)


def _dot_nt(a, b):
    return lax.dot_general(a, b, (((1,), (1,)), ((), ())), preferred_element_type=F32)


def _dot_tn(a, b):
    return lax.dot_general(a, b, (((0,), (0,)), ((), ())), preferred_element_type=F32)


def _row_slices(total, rows):
    return [slice(r * rows, (r + 1) * rows) for r in range(total // rows)]


def _cast_specs(stacked, layer, nsteps, step_of):
    in_specs, out_specs, out_shapes = [], [], []
    for a in stacked:
        _, nrows, ncols = a.shape
        rc = nrows // nsteps
        assert rc * nsteps == nrows and rc % (2 * _V7X_SUBLANES) == 0, (a.shape, nsteps)
        in_specs.append(pl.BlockSpec((None, rc, ncols), lambda *g: (layer, step_of(*g), 0)))
        out_specs.append(pl.BlockSpec((rc, ncols), lambda *g: (step_of(*g), 0)))
        out_shapes.append(jax.ShapeDtypeStruct((nrows, ncols), BF16))
    return in_specs, out_specs, out_shapes


def _cast_chunks(src_refs, dst_refs):
    for src, dst in zip(src_refs, dst_refs):
        dst[...] = src[...].astype(BF16)


def _ffn_kernel(x_ref, g_ref, wg_ref, wu_ref, wo_ref, fg_ref, *rest, rows, tf, final_norm):
    n_cast = (len(rest) - 1) // 2
    o_ref = rest[n_cast]
    _cast_chunks(rest[:n_cast], rest[n_cast + 1:])
    f = wg_ref.shape[1]
    for sl in _row_slices(x_ref.shape[0], rows):
        x = x_ref[sl, :]
        xn = _rmsnorm(x, g_ref[...]).astype(BF16)
        acc = None
        for c in range(f // tf):
            cs = slice(c * tf, (c + 1) * tf)
            gate = _dot(xn, wg_ref[:, cs])
            up = _dot(xn, wu_ref[:, cs])
            act = (gate * _sigmoid(gate) * up).astype(BF16)
            part = _dot(act, wo_ref[cs, :])
            acc = part if acc is None else acc + part
        y = x + 0.5 * acc
        if final_norm:
            y = _rmsnorm(y, fg_ref[...])
        o_ref[sl, :] = y


def _ffn(x2, gain, w_in, w_out, final_gain, *, final_norm, cast=(), cast_layer=0,
         tm=1024, rows=256, tf=1408):
    t, d = x2.shape
    f = w_out.shape[0]
    nsteps = t // tm
    c_in, c_out, c_shapes = _cast_specs(cast, cast_layer, nsteps, lambda i: i)
    outs = pl.pallas_call(
        functools.partial(_ffn_kernel, rows=rows, tf=tf, final_norm=final_norm),
        grid=(nsteps,),
        in_specs=[
            pl.BlockSpec((tm, d), lambda i: (i, 0)),
            _resident((1, d)),
            pl.BlockSpec((d, f), lambda i: (0, 0), pipeline_mode=pl.Buffered(1)),
            pl.BlockSpec((d, f), lambda i: (0, 1), pipeline_mode=pl.Buffered(1)),
            _resident((f, d)), _resident((1, d)),
        ] + c_in,
        out_specs=[pl.BlockSpec((tm, d), lambda i: (i, 0))] + c_out,
        out_shape=[jax.ShapeDtypeStruct((t, d), F32)] + c_shapes,
        compiler_params=_params(1),
        name="ffn",
    )(x2, gain, w_in, w_in, w_out, final_gain, *cast)
    return outs[0], list(outs[1:])


def _mix_in_kernel(x_ref, g_ref, w_ref, cos_ref, sin_ref,
                   hf_ref, q_ref, k_ref, v_ref, sg_ref, gt_ref, *, rows):
    def rotary(hx, out_ref, sl, scale):
        cosf = cos_ref[sl, :]
        sinf = sin_ref[sl, :]
        for h in range(RET_HEADS):
            blk = hx[:, h * RET_DK:(h + 1) * RET_DK]
            rot = blk * cosf + pltpu.roll(blk, RET_DK // 2, 1) * sinf
            if scale != 1.0:
                rot = rot * scale
            out_ref[0, sl, h * RET_DK:(h + 1) * RET_DK] = rot.astype(out_ref.dtype)

    for sl in _row_slices(x_ref.shape[1], rows):
        xn = _rmsnorm(x_ref[0, sl, :], g_ref[...]).astype(BF16)

        def proj(cols):
            return _dot(xn, w_ref[:, cols[0]:cols[1]])

        hf_ref[0, sl, :] = proj(_C_F).astype(hf_ref.dtype)
        rotary(proj(_C_Q), q_ref, sl, 1.0)
        rotary(proj(_C_K), k_ref, sl, RET_DK ** -0.5)
        v_ref[0, sl, :] = proj(_C_V).astype(v_ref.dtype)
        hg = proj(_C_G)
        sg_ref[0, sl, :] = (hg * _sigmoid(hg)).astype(sg_ref.dtype)
        gt_ref[0, sl, :] = _sigmoid(proj(_C_GATES)).astype(gt_ref.dtype)


def _mix_in(x3, gain, w, cosf, sinf, *, hf_dtype, tm=1024, rows=256):
    b, s, d = x3.shape
    n = w.shape[1]

    def tok(width):
        return pl.BlockSpec((1, tm, width), lambda i, j: (i, j, 0))

    def out(width, dtype=BF16):
        return jax.ShapeDtypeStruct((b, s, width), dtype)

    return pl.pallas_call(
        functools.partial(_mix_in_kernel, rows=rows),
        grid=(b, s // tm),
        in_specs=[
            tok(d), _resident((1, d)), _resident((d, n)),
            pl.BlockSpec((tm, RET_DK), lambda i, j: (j, 0)),
            pl.BlockSpec((tm, RET_DK), lambda i, j: (j, 0)),
        ],
        out_specs=[tok(FOURIER_WIDTH), tok(RET_QK_WIDTH), tok(RET_QK_WIDTH),
                   tok(RET_V_WIDTH), tok(RET_V_WIDTH), tok(2 * D_MODEL)],
        out_shape=[out(FOURIER_WIDTH, hf_dtype), out(RET_QK_WIDTH), out(RET_QK_WIDTH),
                   out(RET_V_WIDTH), out(RET_V_WIDTH), out(2 * D_MODEL)],
        compiler_params=_params(2),
        name="mix_in",
    )(x3, gain, w, cosf, sinf)


def _dft_direct_kernel(a_ref, f_ref, xr_ref, xi_ref):
    s = a_ref.shape[1]
    res = _dot(f_ref[...], a_ref[0])
    xr_ref[0] = res[:s].astype(BF16)
    xi_ref[0] = res[s:].astype(BF16)


def _dft_direct(hf, fmat, *, bw=512):
    b, s, c = hf.shape
    spec = pl.BlockSpec((1, s, bw), lambda i, j: (i, 0, j))
    return pl.pallas_call(
        _dft_direct_kernel,
        grid=(b, c // bw),
        in_specs=[spec, _resident((2 * s, s))],
        out_specs=[spec, spec],
        out_shape=[jax.ShapeDtypeStruct((b, s, c), BF16)] * 2,
        compiler_params=_params(2),
        name="dft_direct",
    )(hf, fmat)


def _lane_blocks(c):
    return [slice(q * _V7X_LANES, (q + 1) * _V7X_LANES) for q in range(c // _V7X_LANES)]


def _pack_pair(re, im):
    hi = lax.bitcast_convert_type(re.astype(BF16).astype(F32), jnp.uint32)
    lo = lax.bitcast_convert_type(im.astype(BF16).astype(F32), jnp.uint32)
    return hi | (lo >> 16)


def _unpack_pair(w):
    re = lax.bitcast_convert_type(w & jnp.uint32(0xFFFF0000), F32)
    im = lax.bitcast_convert_type(w << 16, F32)
    return re, im


def _dft_stage1_kernel(a_ref, f_ref, tc_ref, ts_ref, t_ref, a2, t2):
    n1 = a_ref.shape[1]
    slab = (n1, _V7X_SUBLANES, _V7X_LANES)
    for q, ls in enumerate(_lane_blocks(a_ref.shape[-1])):
        a2[q] = a_ref[0, :, :, ls].reshape(a2.shape[1:])
        for j in range(_V7X_SUBLANES):
            pick = pl.ds(j, n1, stride=_V7X_SUBLANES)
            res = _dot(f_ref[...], a2[q, pick, :].astype(BF16))
            ur, ui = res[:n1], res[n1:]
            cw, sw = tc_ref[j], ts_ref[j]
            t2[q, pick, :] = _pack_pair(ur * cw + ui * sw, ui * cw - ur * sw)
        t_ref[0, :, :, ls] = t2[q].reshape(slab)


def _dft_stage1(hf, f1, twc, tws, *, n1):
    b, s, c = hf.shape
    n2 = s // n1
    blk = pl.BlockSpec((1, n1, _V7X_SUBLANES, c), lambda i, j: (i, 0, j, 0))
    tw = pl.BlockSpec((_V7X_SUBLANES, n1, _V7X_LANES), lambda i, j: (j, 0, 0))
    rows = (c // _V7X_LANES, n1 * _V7X_SUBLANES, _V7X_LANES)
    return pl.pallas_call(
        _dft_stage1_kernel,
        grid=(b, n2 // _V7X_SUBLANES),
        in_specs=[blk, _resident((2 * n1, n1)), tw, tw],
        out_specs=blk,
        out_shape=jax.ShapeDtypeStruct((b, n1, n2, c), jnp.uint32),
        scratch_shapes=[pltpu.VMEM(rows, F32), pltpu.VMEM(rows, jnp.uint32)],
        compiler_params=_params(2),
        name="dft_stage1",
    )(hf.reshape(b, n1, n2, c), f1, twc, tws)


def _dft_stage2_kernel(t_ref, f_ref, x_ref, x2):
    n2 = t_ref.shape[2]
    slab = (n2, _V7X_SUBLANES, _V7X_LANES)
    for q, ls in enumerate(_lane_blocks(t_ref.shape[-1])):
        for j in range(_V7X_SUBLANES):
            t = jnp.concatenate(_unpack_pair(t_ref[0, j, :, ls]), axis=0).astype(BF16)
            res = _dot(f_ref[...], t)
            pick = pl.ds(j, n2, stride=_V7X_SUBLANES)
            x2[q, pick, :] = _pack_pair(res[:n2], res[n2:])
        x_ref[0, :, :, ls] = x2[q].reshape(slab)


def _dft_stage2(t, f2):
    b, n1, n2, c = t.shape
    inb = pl.BlockSpec((1, _V7X_SUBLANES, n2, c), lambda i, j: (i, j, 0, 0))
    outb = pl.BlockSpec((1, n2, _V7X_SUBLANES, c), lambda i, j: (i, 0, j, 0))
    rows = (c // _V7X_LANES, n2 * _V7X_SUBLANES, _V7X_LANES)
    xp = pl.pallas_call(
        _dft_stage2_kernel,
        grid=(b, n1 // _V7X_SUBLANES),
        in_specs=[inb, _resident((2 * n2, 2 * n2))],
        out_specs=outb,
        out_shape=jax.ShapeDtypeStruct((b, n2, n1, c), jnp.uint32),
        scratch_shapes=[pltpu.VMEM(rows, jnp.uint32)],
        compiler_params=_params(2),
        name="dft_stage2",
    )(t, f2)
    return xp.reshape(b, n2 * n1, c)


def _retention_kernel(dec_ref, q_ref, k_ref, v_ref, sg_ref, o_ref,
                      dm_ref, xif_ref, xib_ref, zf_ref, zb_ref, gcf_ref, gcb_ref, st_ref, rb_ref,
                      *, cc, nch, ns):
    b = pl.program_id(0)
    t = pl.program_id(1)

    @pl.when((b == 0) & (t == 0))
    def _():
        row = lax.broadcasted_iota(jnp.int32, (cc, cc), 0)
        col = lax.broadcasted_iota(jnp.int32, (cc, cc), 1)
        diff = (row - col).astype(F32)
        rowi = lax.broadcasted_iota(jnp.int32, (cc, RET_DV), 0).astype(F32)
        for h in range(RET_HEADS):
            lgf = _log_sigmoid(dec_ref[0, h])
            lgb = _log_sigmoid(dec_ref[1, h])
            dm_ref[h] = jnp.where(diff >= 0.0,
                                  jnp.exp(lgf * jnp.maximum(diff, 0.0)),
                                  jnp.exp(lgb * jnp.maximum(-diff, 0.0)))
            xif_ref[h] = jnp.exp(lgf * (rowi + 1.0))[:, :RET_DK]
            xib_ref[h] = jnp.exp(lgb * (float(cc) - rowi))[:, :RET_DK]
            zf_ref[h] = jnp.exp(lgf * (float(cc) - 1.0 - rowi))[:, :RET_DK]
            zb_ref[h] = jnp.exp(lgb * rowi)[:, :RET_DK]
            gcf_ref[h] = jnp.exp(lgf * float(cc))
            gcb_ref[h] = jnp.exp(lgb * float(cc))

    @pl.when((t == 0) | (t == ns))
    def _():
        st_ref[...] = jnp.zeros_like(st_ref)

    @pl.when(t < ns)
    def _():
        first = (ns - 1 - t) * nch
        for h in range(RET_HEADS):
            st = st_ref[h]
            for c in reversed(range(nch)):
                rs = slice(c * cc, (c + 1) * cc)
                rb_ref[first + c, h * RET_DK:(h + 1) * RET_DK, :] = st.astype(BF16)
                kh = k_ref[0, rs, h * RET_DK:(h + 1) * RET_DK]
                vh = v_ref[0, rs, h * RET_DV:(h + 1) * RET_DV]
                kz = (kh.astype(F32) * zb_ref[h]).astype(BF16)
                st = gcb_ref[h] * st + _dot_tn(kz, vh)
            st_ref[h] = st

    @pl.when(t >= ns)
    def _():
        first = (t - ns) * nch
        for h in range(RET_HEADS):
            st = st_ref[h]
            for c in range(nch):
                rs = slice(c * cc, (c + 1) * cc)
                qh = q_ref[0, rs, h * RET_DK:(h + 1) * RET_DK]
                kh = k_ref[0, rs, h * RET_DK:(h + 1) * RET_DK]
                vh = v_ref[0, rs, h * RET_DV:(h + 1) * RET_DV]
                p = (_dot_nt(qh, kh) * dm_ref[h]).astype(BF16)
                qf32 = qh.astype(F32)
                qf = (qf32 * xif_ref[h]).astype(BF16)
                qb = (qf32 * xib_ref[h]).astype(BF16)
                lhs = jnp.concatenate([p, qf, qb], axis=1)
                rhs = jnp.concatenate(
                    [vh, st.astype(BF16), rb_ref[first + c, h * RET_DK:(h + 1) * RET_DK, :]], axis=0)
                y = _dot(lhs, rhs)
                kz = (kh.astype(F32) * zf_ref[h]).astype(BF16)
                st = gcf_ref[h] * st + _dot_tn(kz, vh)
                ms = jnp.mean(y * y, axis=-1, keepdims=True)
                yn = y * lax.rsqrt(ms + EPS)
                sg = sg_ref[0, rs, h * RET_DV:(h + 1) * RET_DV].astype(F32)
                o_ref[0, rs, h * RET_DV:(h + 1) * RET_DV] = (sg * yn).astype(BF16)
            st_ref[h] = st


def _retention(dec, q, k, v, sg, *, cc, nch=4):
    b, s, _ = q.shape
    nch = min(nch, s // cc)
    ns = s // (cc * nch)

    def both(width):
        return pl.BlockSpec((1, cc * nch, width),
                            lambda i, t: (i, jnp.where(t < ns, ns - 1 - t, t - ns), 0))

    def fwd(width):
        return pl.BlockSpec((1, cc * nch, width), lambda i, t: (i, jnp.maximum(t - ns, 0), 0))

    vec = pltpu.VMEM((RET_HEADS, cc, RET_DK), F32)
    gain = pltpu.VMEM((RET_HEADS, 1, RET_DV), F32)
    return pl.pallas_call(
        functools.partial(_retention_kernel, cc=cc, nch=nch, ns=ns),
        grid=(b, 2 * ns),
        in_specs=[_resident(dec.shape), fwd(RET_QK_WIDTH), both(RET_QK_WIDTH), both(RET_V_WIDTH),
                  fwd(RET_V_WIDTH)],
        out_specs=fwd(RET_V_WIDTH),
        out_shape=jax.ShapeDtypeStruct((b, s, RET_V_WIDTH), BF16),
        scratch_shapes=[
            pltpu.VMEM((RET_HEADS, cc, cc), F32), vec, vec, vec, vec, gain, gain,
            pltpu.VMEM((RET_HEADS, RET_DK, RET_DV), F32),
            pltpu.VMEM((s // cc, RET_QK_WIDTH, RET_DV), BF16),
        ],
        compiler_params=_params(2),
        name="retention",
    )(dec, q, k, v, sg)


def _mix_attn_kernel(x_ref, *rest, scale, rows):
    spectrum = rest[:-11]
    (yr_ref, gt_ref, cs_ref, fw_ref, rw_ref, mw_ref, g_ref, wq_ref, kv_ref, wo_ref, o_ref) = rest[-11:]
    d = x_ref.shape[-1]
    heads = [(h * XA_HEAD_DIM, (h + 1) * XA_HEAD_DIM) for h in range(XA_HEADS)]
    sls = _row_slices(x_ref.shape[1], rows)
    xs = []
    for sl in sls:
        groups = []
        for ls in _lane_blocks(spectrum[0].shape[-1]):
            if len(spectrum) == 1:
                pair = _unpack_pair(spectrum[0][0, sl, ls])
            else:
                pair = (spectrum[0][0, sl, ls], spectrum[1][0, sl, ls])
            groups.append(_dot(jnp.concatenate(pair, axis=1).astype(BF16), cs_ref[...]))
        yf = jnp.concatenate(groups, axis=1) * scale
        ya = _dot(yf.astype(BF16), fw_ref[...])
        yb = _dot(yr_ref[0, sl, :], rw_ref[...])
        ga = gt_ref[0, sl, :d].astype(F32)
        gb = gt_ref[0, sl, d:].astype(F32)
        merged = (ga * ya + gb * yb).astype(BF16)
        xs.append(x_ref[0, sl, :] + _dot(merged, mw_ref[...]))
    qs = [_dot(_rmsnorm(x, g_ref[...]).astype(BF16), wq_ref[...]).astype(BF16) for x in xs]
    scores = [[_dot_nt(q[:, lo:hi], kv_ref[0, :, lo:hi]) * (XA_HEAD_DIM ** -0.5) for lo, hi in heads]
              for q in qs]
    probs = []
    for per_head in scores:
        row = []
        for s in per_head:
            e = jnp.exp(s - jnp.max(s, axis=-1, keepdims=True))
            row.append((e / jnp.sum(e, axis=-1, keepdims=True)).astype(BF16))
        probs.append(row)
    outs = [[_dot(p, kv_ref[0, :, d + lo:d + hi]).astype(BF16) for p, (lo, hi) in zip(row, heads)]
            for row in probs]
    for sl, x, per_head in zip(sls, xs, outs):
        o_ref[0, sl, :] = x + _dot(jnp.concatenate(per_head, axis=-1), wo_ref[...])


def _mix_attn(x3, spectrum, yr, gt, cs, fw, rw, mw, gain, wq, kv, wo, *, kv_at, scale,
              tm=1024, rows=512):
    b, s, d = x3.shape
    m = kv.shape[2]
    layer, seq0 = kv_at

    def tok(width):
        return pl.BlockSpec((1, tm, width), lambda i, j: (i, j, 0))

    return pl.pallas_call(
        functools.partial(_mix_attn_kernel, scale=scale, rows=rows),
        grid=(b, s // tm),
        in_specs=[tok(d)] + [tok(FOURIER_WIDTH)] * len(spectrum) + [
            tok(RET_V_WIDTH), tok(2 * d),
            _resident(cs.shape), _resident(fw.shape), _resident(rw.shape), _resident(mw.shape),
            _resident((1, d)), _resident((d, d)),
            pl.BlockSpec((None, 1, m, 2 * d), lambda i, j: (layer, seq0 + i, 0, 0)),
            _resident((d, d))],
        out_specs=tok(d),
        out_shape=jax.ShapeDtypeStruct((b, s, d), F32),
        compiler_params=_params(2),
        name="mix_attn",
    )(x3, *spectrum, yr, gt, cs, fw, rw, mw, gain, wq, kv, wo)


def _mem_kv_kernel(m_ref, g_ref, w_ref, o_ref):
    mn = _rmsnorm(m_ref[...], g_ref[...]).astype(BF16)
    o_ref[...] = _dot(mn, w_ref[...].astype(BF16)).astype(BF16)


def _mem_kv(mem2, gains, wkv, *, steps=3):
    t, d = mem2.shape
    depth, _, n = wkv.shape
    tm = t // steps
    assert tm * steps == t and tm % (2 * _V7X_SUBLANES) == 0, (t, steps)
    return pl.pallas_call(
        _mem_kv_kernel,
        grid=(depth, steps),
        in_specs=[pl.BlockSpec((tm, d), lambda l, i: (i, 0)),
                  pl.BlockSpec((None, 1, d), lambda l, i: (l, 0, 0)),
                  pl.BlockSpec((None, d, n), lambda l, i: (l, 0, 0))],
        out_specs=pl.BlockSpec((None, tm, n), lambda l, i: (l, i, 0)),
        out_shape=jax.ShapeDtypeStruct((depth, t, n), BF16),
        compiler_params=_params(2),
        name="mem_kv",
    )(mem2, gains, wkv)


def _angle(num, n):
    return (2.0 * math.pi / n) * (num % n).astype(F32)


_DFT_TABLE_SPLIT = 32
_DFT_TABLE_DIRECT_MAX = 1024


def _dft_cos_sin(n):
    i = jnp.arange(n, dtype=jnp.int32)
    if n <= _DFT_TABLE_DIRECT_MAX:
        ang = _angle(i[:, None] * i[None, :], n)
        return jnp.cos(ang), jnp.sin(ang)
    na = n // _DFT_TABLE_SPLIT
    ang_a = _angle(i[:, None] * jnp.arange(na, dtype=jnp.int32)[None, :], na)[:, :, None]
    ang_b = _angle(i[:, None] * jnp.arange(_DFT_TABLE_SPLIT, dtype=jnp.int32)[None, :], n)[:, None, :]
    ca, sa, cb, sb = lax.optimization_barrier(
        (jnp.cos(ang_a), jnp.sin(ang_a), jnp.cos(ang_b), jnp.sin(ang_b)))
    return (ca * cb - sa * sb).reshape(n, n), (sa * cb + ca * sb).reshape(n, n)


def _rotary_tables(s):
    half = RET_DK // 2
    inv = 1.0 / (ROPE_BASE ** (jnp.arange(half, dtype=F32) * 2.0 / RET_DK))
    ang = jnp.arange(s, dtype=F32)[:, None] * inv[None, :]
    cos, sin = jnp.cos(ang), jnp.sin(ang)
    return jnp.concatenate([cos, cos], -1), jnp.concatenate([-sin, sin], -1)


def _seq_tables(s):
    tabs = {"rot": _rotary_tables(s)}
    if s >= _DFT_TWO_STAGE_MIN:
        n1, n2 = _DFT_N1, s // _DFT_N1
        c1, s1 = _dft_cos_sin(n1)
        tabs["f1"] = jnp.concatenate([c1, -s1], 0).astype(BF16)
        c2, s2 = _dft_cos_sin(n2)
        tabs["f2"] = jnp.concatenate(
            [jnp.concatenate([c2, s2], 1), jnp.concatenate([-s2, c2], 1)], 0).astype(BF16)
        ang = _angle(jnp.arange(n2, dtype=jnp.int32)[:, None] * jnp.arange(n1, dtype=jnp.int32)[None, :], s)
        lane = (n2, n1, _V7X_LANES)
        tabs["tw"] = (jnp.broadcast_to(jnp.cos(ang)[:, :, None], lane),
                      jnp.broadcast_to(jnp.sin(ang)[:, :, None], lane))
    else:
        c, sn = _dft_cos_sin(s)
        tabs["f"] = jnp.concatenate([c, -sn], 0).astype(BF16)
    return tabs


def _channel_dft_matrix():
    c, s = _dft_cos_sin(FOURIER_GROUP_DIM)
    return jnp.concatenate([c, s], 0).astype(BF16)


_CAST_BY_FFN1 = ("mix_w_in", "ffn2_w_in", "ffn2_w_out")
_CAST_BY_FFN2 = ("ffn1_w_in", "ffn1_w_out", "fourier_w", "ret_w_out", "mix_w_out", "xa_wq", "xa_wo")


def _trunk(x, kv, seq0, small, wb, bd, stacked=None):
    b, s, d = x.shape
    tabs = _seq_tables(s)
    cosf, sinf = tabs["rot"]
    two_stage = "f" not in tabs
    scale = float((s * FOURIER_GROUP_DIM) ** -0.5)
    x2 = x.reshape(b * s, d)
    for l in range(DEPTH):
        sm, lw = small[l], wb.setdefault(l, {})

        def cast_args(names, layer):
            if stacked is None or layer >= DEPTH:
                return {}
            return {"cast": [stacked[n] for n in names], "cast_layer": layer}

        def keep(names, copies, layer):
            if copies:
                wb.setdefault(layer, {}).update(zip(names, copies))

        x2, copies = _ffn(x2, sm["ffn1_norm"], lw["ffn1_w_in"], lw["ffn1_w_out"], small["final_norm"],
                          final_norm=False, **cast_args(_CAST_BY_FFN1, l))
        keep(_CAST_BY_FFN1, copies, l)
        hf, q, k, v, sg, gt = _mix_in(x2.reshape(b, s, d), sm["mix_norm"], lw["mix_w_in"], cosf, sinf,
                                      hf_dtype=F32 if two_stage else BF16)
        if two_stage:
            spectrum = (_dft_stage2(_dft_stage1(hf, tabs["f1"], *tabs["tw"], n1=_DFT_N1), tabs["f2"]),)
        else:
            spectrum = _dft_direct(hf, tabs["f"])
        yr = _retention(sm["dec"], q, k, v, sg, cc=_RET_CHUNK)
        x3 = _mix_attn(x2.reshape(b, s, d), spectrum, yr, gt, bd, lw["fourier_w"], lw["ret_w_out"],
                       lw["mix_w_out"], sm["xa_norm"], lw["xa_wq"], kv, lw["xa_wo"],
                       kv_at=(l, seq0), scale=scale)
        x2, copies = _ffn(x3.reshape(b * s, d), sm["ffn2_norm"], lw["ffn2_w_in"], lw["ffn2_w_out"],
                          small["final_norm"], final_norm=(l == DEPTH - 1),
                          **cast_args(_CAST_BY_FFN2, l + 1))
        keep(_CAST_BY_FFN2, copies, l + 1)
    return x2.reshape(b, s, d)


def kernel(x_prompt, x_sample, mem_prompt, mem_sample, ffn1_norm, ffn1_w_in, ffn1_w_out, mix_norm, mix_w_in, fourier_w, ret_decay_fwd, ret_decay_bwd, ret_w_out, mix_w_out, xa_norm, mem_norm, xa_wq, xa_wkv, xa_wo, ffn2_norm, ffn2_w_in, ffn2_w_out, final_norm):
    row = lambda a: a.reshape(1, -1)
    small = {"final_norm": row(final_norm)}
    for l in range(DEPTH):
        dec = jnp.stack([ret_decay_fwd[l], ret_decay_bwd[l]])
        small[l] = {
            "ffn1_norm": row(ffn1_norm[l]), "mix_norm": row(mix_norm[l]), "xa_norm": row(xa_norm[l]),
            "ffn2_norm": row(ffn2_norm[l]),
            "dec": jnp.broadcast_to(dec[:, :, None, None], (2, RET_HEADS, 1, RET_DV)),
        }
    stacked = {
        "ffn1_w_in": ffn1_w_in, "ffn1_w_out": ffn1_w_out, "mix_w_in": mix_w_in, "fourier_w": fourier_w,
        "ret_w_out": ret_w_out, "mix_w_out": mix_w_out, "xa_wq": xa_wq, "xa_wo": xa_wo,
        "ffn2_w_in": ffn2_w_in, "ffn2_w_out": ffn2_w_out,
    }
    wb = {0: {name: stacked[name][0].astype(BF16) for name in _CAST_BY_FFN2}}
    bd = _channel_dft_matrix()
    d = x_prompt.shape[-1]
    m = mem_prompt.shape[1]
    mem_all = jnp.concatenate([mem_prompt.reshape(-1, d), mem_sample.reshape(-1, d)])
    kv = _mem_kv(mem_all, mem_norm[:, None, :], xa_wkv).reshape(DEPTH, -1, m, 2 * d)
    y_prompt = _trunk(x_prompt, kv, 0, small, wb, bd, stacked)
    y_sample = _trunk(x_sample, kv, mem_prompt.shape[0], small, wb, bd)
    return (y_prompt, y_sample)
```

```python
import functools
import math

import jax
import jax.numpy as jnp
from jax import lax
from jax.experimental import pallas as pl
from jax.experimental.pallas import tpu as pltpu

F32 = jnp.float32
BF16 = jnp.bfloat16

D_MODEL = 1024
DEPTH = 4
FOURIER_GROUP_DIM = 128
FOURIER_WIDTH = 512
RET_HEADS = 4
RET_DK = 128
RET_DV = 256
RET_QK_WIDTH = RET_HEADS * RET_DK
RET_V_WIDTH = RET_HEADS * RET_DV
XA_HEADS = 4
XA_HEAD_DIM = D_MODEL // XA_HEADS
ROPE_BASE = 10000.0
EPS = 1e-6

_C_F = (0, FOURIER_WIDTH)
_C_Q = (_C_F[1], _C_F[1] + RET_QK_WIDTH)
_C_K = (_C_Q[1], _C_Q[1] + RET_QK_WIDTH)
_C_V = (_C_K[1], _C_K[1] + RET_V_WIDTH)
_C_G = (_C_V[1], _C_V[1] + RET_V_WIDTH)
_C_GATES = (_C_G[1], _C_G[1] + 2 * D_MODEL)

_V7X_VMEM_BYTES = 64 * 1024 * 1024
_VMEM_LIMIT = _V7X_VMEM_BYTES - 8 * 1024 * 1024
_V7X_LANES = 128
_V7X_SUBLANES = 8

_RET_CHUNK = 256
_DFT_TWO_STAGE_MIN = 16384
_DFT_N1 = 128


def _params(n_axes):
    return pltpu.CompilerParams(
        dimension_semantics=("arbitrary",) * n_axes, vmem_limit_bytes=_VMEM_LIMIT)


def _resident(shape):
    nd = len(shape)
    return pl.BlockSpec(shape, lambda *_: (0,) * nd, pipeline_mode=pl.Buffered(1))


def _rmsnorm(x, g):
    ms = jnp.mean(x * x, axis=-1, keepdims=True)
    return x * lax.rsqrt(ms + EPS) * g


def _sigmoid(x):
    return 1.0 / (1.0 + jnp.exp(-x))


def _log_sigmoid(x):
    return -(jnp.maximum(-x, 0.0) + jnp.log1p(jnp.exp(-jnp.abs(x))))


def _dot(a, b):
    return jnp.dot(a, b, preferred_element_type=F32)


def _dot_nt(a, b):
    return lax.dot_general(a, b, (((1,), (1,)), ((), ())), preferred_element_type=F32)


def _dot_tn(a, b):
    return lax.dot_general(a, b, (((0,), (0,)), ((), ())), preferred_element_type=F32)


def _row_slices(total, rows):
    return [slice(r * rows, (r + 1) * rows) for r in range(total // rows)]


def _cast_specs(stacked, layer, nsteps, step_of):
    in_specs, out_specs, out_shapes = [], [], []
    for a in stacked:
        _, nrows, ncols = a.shape
        rc = nrows // nsteps
        assert rc * nsteps == nrows and rc % (2 * _V7X_SUBLANES) == 0, (a.shape, nsteps)
        in_specs.append(pl.BlockSpec((None, rc, ncols), lambda *g: (layer, step_of(*g), 0)))
        out_specs.append(pl.BlockSpec((rc, ncols), lambda *g: (step_of(*g), 0)))
        out_shapes.append(jax.ShapeDtypeStruct((nrows, ncols), BF16))
    return in_specs, out_specs, out_shapes


def _cast_chunks(src_refs, dst_refs):
    for src, dst in zip(src_refs, dst_refs):
        dst[...] = src[...].astype(BF16)


def _ffn_kernel(x_ref, g_ref, wg_ref, wu_ref, wo_ref, fg_ref, *rest, rows, tf, final_norm):
    n_cast = (len(rest) - 1) // 2
    o_ref = rest[n_cast]
    _cast_chunks(rest[:n_cast], rest[n_cast + 1:])
    f = wg_ref.shape[1]
    for sl in _row_slices(x_ref.shape[0], rows):
        x = x_ref[sl, :]
        xn = _rmsnorm(x, g_ref[...]).astype(BF16)
        acc = None
        for c in range(f // tf):
            cs = slice(c * tf, (c + 1) * tf)
            gate = _dot(xn, wg_ref[:, cs])
            up = _dot(xn, wu_ref[:, cs])
            act = (gate * _sigmoid(gate) * up).astype(BF16)
            part = _dot(act, wo_ref[cs, :])
            acc = part if acc is None else acc + part
        y = x + 0.5 * acc
        if final_norm:
            y = _rmsnorm(y, fg_ref[...])
        o_ref[sl, :] = y


def _ffn(x2, gain, w_in, w_out, final_gain, *, final_norm, cast=(), cast_layer=0,
         tm=1024, rows=256, tf=1408):
    t, d = x2.shape
    f = w_out.shape[0]
    nsteps = t // tm
    c_in, c_out, c_shapes = _cast_specs(cast, cast_layer, nsteps, lambda i: i)
    outs = pl.pallas_call(
        functools.partial(_ffn_kernel, rows=rows, tf=tf, final_norm=final_norm),
        grid=(nsteps,),
        in_specs=[
            pl.BlockSpec((tm, d), lambda i: (i, 0)),
            _resident((1, d)),
            pl.BlockSpec((d, f), lambda i: (0, 0), pipeline_mode=pl.Buffered(1)),
            pl.BlockSpec((d, f), lambda i: (0, 1), pipeline_mode=pl.Buffered(1)),
            _resident((f, d)), _resident((1, d)),
        ] + c_in,
        out_specs=[pl.BlockSpec((tm, d), lambda i: (i, 0))] + c_out,
        out_shape=[jax.ShapeDtypeStruct((t, d), F32)] + c_shapes,
        compiler_params=_params(1),
        name="ffn",
    )(x2, gain, w_in, w_in, w_out, final_gain, *cast)
    return outs[0], list(outs[1:])


def _mix_in_kernel(x_ref, g_ref, w_ref, cos_ref, sin_ref,
                   hf_ref, q_ref, k_ref, v_ref, sg_ref, gt_ref, *, rows):
    def rotary(hx, out_ref, sl, scale):
        cosf = cos_ref[sl, :]
        sinf = sin_ref[sl, :]
        for h in range(RET_HEADS):
            blk = hx[:, h * RET_DK:(h + 1) * RET_DK]
            rot = blk * cosf + pltpu.roll(blk, RET_DK // 2, 1) * sinf
            if scale != 1.0:
                rot = rot * scale
            out_ref[0, sl, h * RET_DK:(h + 1) * RET_DK] = rot.astype(out_ref.dtype)

    for sl in _row_slices(x_ref.shape[1], rows):
        xn = _rmsnorm(x_ref[0, sl, :], g_ref[...]).astype(BF16)

        def proj(cols):
            return _dot(xn, w_ref[:, cols[0]:cols[1]])

        hf_ref[0, sl, :] = proj(_C_F).astype(hf_ref.dtype)
        rotary(proj(_C_Q), q_ref, sl, 1.0)
        rotary(proj(_C_K), k_ref, sl, RET_DK ** -0.5)
        v_ref[0, sl, :] = proj(_C_V).astype(v_ref.dtype)
        hg = proj(_C_G)
        sg_ref[0, sl, :] = (hg * _sigmoid(hg)).astype(sg_ref.dtype)
        gt_ref[0, sl, :] = _sigmoid(proj(_C_GATES)).astype(gt_ref.dtype)


def _mix_in(x3, gain, w, cosf, sinf, *, hf_dtype, tm=1024, rows=256):
    b, s, d = x3.shape
    n = w.shape[1]

    def tok(width):
        return pl.BlockSpec((1, tm, width), lambda i, j: (i, j, 0))

    def out(width, dtype=BF16):
        return jax.ShapeDtypeStruct((b, s, width), dtype)

    return pl.pallas_call(
        functools.partial(_mix_in_kernel, rows=rows),
        grid=(b, s // tm),
        in_specs=[
            tok(d), _resident((1, d)), _resident((d, n)),
            pl.BlockSpec((tm, RET_DK), lambda i, j: (j, 0)),
            pl.BlockSpec((tm, RET_DK), lambda i, j: (j, 0)),
        ],
        out_specs=[tok(FOURIER_WIDTH), tok(RET_QK_WIDTH), tok(RET_QK_WIDTH),
                   tok(RET_V_WIDTH), tok(RET_V_WIDTH), tok(2 * D_MODEL)],
        out_shape=[out(FOURIER_WIDTH, hf_dtype), out(RET_QK_WIDTH), out(RET_QK_WIDTH),
                   out(RET_V_WIDTH), out(RET_V_WIDTH), out(2 * D_MODEL)],
        compiler_params=_params(2),
        name="mix_in",
    )(x3, gain, w, cosf, sinf)


def _dft_direct_kernel(a_ref, f_ref, xr_ref, xi_ref):
    s = a_ref.shape[1]
    res = _dot(f_ref[...], a_ref[0])
    xr_ref[0] = res[:s].astype(BF16)
    xi_ref[0] = res[s:].astype(BF16)


def _dft_direct(hf, fmat, *, bw=512):
    b, s, c = hf.shape
    spec = pl.BlockSpec((1, s, bw), lambda i, j: (i, 0, j))
    return pl.pallas_call(
        _dft_direct_kernel,
        grid=(b, c // bw),
        in_specs=[spec, _resident((2 * s, s))],
        out_specs=[spec, spec],
        out_shape=[jax.ShapeDtypeStruct((b, s, c), BF16)] * 2,
        compiler_params=_params(2),
        name="dft_direct",
    )(hf, fmat)


def _lane_blocks(c):
    return [slice(q * _V7X_LANES, (q + 1) * _V7X_LANES) for q in range(c // _V7X_LANES)]


def _pack_pair(re, im):
    hi = lax.bitcast_convert_type(re.astype(BF16).astype(F32), jnp.uint32)
    lo = lax.bitcast_convert_type(im.astype(BF16).astype(F32), jnp.uint32)
    return hi | (lo >> 16)


def _unpack_pair(w):
    re = lax.bitcast_convert_type(w & jnp.uint32(0xFFFF0000), F32)
    im = lax.bitcast_convert_type(w << 16, F32)
    return re, im


def _dft_stage1_kernel(a_ref, f_ref, tc_ref, ts_ref, t_ref, a2, t2):
    n1 = a_ref.shape[1]
    slab = (n1, _V7X_SUBLANES, _V7X_LANES)
    for q, ls in enumerate(_lane_blocks(a_ref.shape[-1])):
        a2[q] = a_ref[0, :, :, ls].reshape(a2.shape[1:])
        for j in range(_V7X_SUBLANES):
            pick = pl.ds(j, n1, stride=_V7X_SUBLANES)
            res = _dot(f_ref[...], a2[q, pick, :].astype(BF16))
            ur, ui = res[:n1], res[n1:]
            cw, sw = tc_ref[j], ts_ref[j]
            t2[q, pick, :] = _pack_pair(ur * cw + ui * sw, ui * cw - ur * sw)
        t_ref[0, :, :, ls] = t2[q].reshape(slab)


def _dft_stage1(hf, f1, twc, tws, *, n1):
    b, s, c = hf.shape
    n2 = s // n1
    blk = pl.BlockSpec((1, n1, _V7X_SUBLANES, c), lambda i, j: (i, 0, j, 0))
    tw = pl.BlockSpec((_V7X_SUBLANES, n1, _V7X_LANES), lambda i, j: (j, 0, 0))
    rows = (c // _V7X_LANES, n1 * _V7X_SUBLANES, _V7X_LANES)
    return pl.pallas_call(
        _dft_stage1_kernel,
        grid=(b, n2 // _V7X_SUBLANES),
        in_specs=[blk, _resident((2 * n1, n1)), tw, tw],
        out_specs=blk,
        out_shape=jax.ShapeDtypeStruct((b, n1, n2, c), jnp.uint32),
        scratch_shapes=[pltpu.VMEM(rows, F32), pltpu.VMEM(rows, jnp.uint32)],
        compiler_params=_params(2),
        name="dft_stage1",
    )(hf.reshape(b, n1, n2, c), f1, twc, tws)


def _dft_stage2_kernel(t_ref, f_ref, x_ref, x2):
    n2 = t_ref.shape[2]
    slab = (n2, _V7X_SUBLANES, _V7X_LANES)
    for q, ls in enumerate(_lane_blocks(t_ref.shape[-1])):
        for j in range(_V7X_SUBLANES):
            t = jnp.concatenate(_unpack_pair(t_ref[0, j, :, ls]), axis=0).astype(BF16)
            res = _dot(f_ref[...], t)
            pick = pl.ds(j, n2, stride=_V7X_SUBLANES)
            x2[q, pick, :] = _pack_pair(res[:n2], res[n2:])
        x_ref[0, :, :, ls] = x2[q].reshape(slab)


def _dft_stage2(t, f2):
    b, n1, n2, c = t.shape
    inb = pl.BlockSpec((1, _V7X_SUBLANES, n2, c), lambda i, j: (i, j, 0, 0))
    outb = pl.BlockSpec((1, n2, _V7X_SUBLANES, c), lambda i, j: (i, 0, j, 0))
    rows = (c // _V7X_LANES, n2 * _V7X_SUBLANES, _V7X_LANES)
    xp = pl.pallas_call(
        _dft_stage2_kernel,
        grid=(b, n1 // _V7X_SUBLANES),
        in_specs=[inb, _resident((2 * n2, 2 * n2))],
        out_specs=outb,
        out_shape=jax.ShapeDtypeStruct((b, n2, n1, c), jnp.uint32),
        scratch_shapes=[pltpu.VMEM(rows, jnp.uint32)],
        compiler_params=_params(2),
        name="dft_stage2",
    )(t, f2)
    return xp.reshape(b, n2 * n1, c)


def _retention_kernel(dec_ref, q_ref, k_ref, v_ref, sg_ref, o_ref,
                      dm_ref, xif_ref, xib_ref, zf_ref, zb_ref, gcf_ref, gcb_ref, st_ref, rb_ref,
                      *, cc, nch, ns):
    b = pl.program_id(0)
    t = pl.program_id(1)

    @pl.when((b == 0) & (t == 0))
    def _():
        row = lax.broadcasted_iota(jnp.int32, (cc, cc), 0)
        col = lax.broadcasted_iota(jnp.int32, (cc, cc), 1)
        diff = (row - col).astype(F32)
        rowi = lax.broadcasted_iota(jnp.int32, (cc, RET_DV), 0).astype(F32)
        for h in range(RET_HEADS):
            lgf = _log_sigmoid(dec_ref[0, h])
            lgb = _log_sigmoid(dec_ref[1, h])
            dm_ref[h] = jnp.where(diff >= 0.0,
                                  jnp.exp(lgf * jnp.maximum(diff, 0.0)),
                                  jnp.exp(lgb * jnp.maximum(-diff, 0.0)))
            xif_ref[h] = jnp.exp(lgf * (rowi + 1.0))[:, :RET_DK]
            xib_ref[h] = jnp.exp(lgb * (float(cc) - rowi))[:, :RET_DK]
            zf_ref[h] = jnp.exp(lgf * (float(cc) - 1.0 - rowi))[:, :RET_DK]
            zb_ref[h] = jnp.exp(lgb * rowi)[:, :RET_DK]
            gcf_ref[h] = jnp.exp(lgf * float(cc))
            gcb_ref[h] = jnp.exp(lgb * float(cc))

    @pl.when((t == 0) | (t == ns))
    def _():
        st_ref[...] = jnp.zeros_like(st_ref)

    @pl.when(t < ns)
    def _():
        first = (ns - 1 - t) * nch
        for h in range(RET_HEADS):
            st = st_ref[h]
            for c in reversed(range(nch)):
                rs = slice(c * cc, (c + 1) * cc)
                rb_ref[first + c, h * RET_DK:(h + 1) * RET_DK, :] = st.astype(BF16)
                kh = k_ref[0, rs, h * RET_DK:(h + 1) * RET_DK]
                vh = v_ref[0, rs, h * RET_DV:(h + 1) * RET_DV]
                kz = (kh.astype(F32) * zb_ref[h]).astype(BF16)
                st = gcb_ref[h] * st + _dot_tn(kz, vh)
            st_ref[h] = st

    @pl.when(t >= ns)
    def _():
        first = (t - ns) * nch
        for h in range(RET_HEADS):
            st = st_ref[h]
            for c in range(nch):
                rs = slice(c * cc, (c + 1) * cc)
                qh = q_ref[0, rs, h * RET_DK:(h + 1) * RET_DK]
                kh = k_ref[0, rs, h * RET_DK:(h + 1) * RET_DK]
                vh = v_ref[0, rs, h * RET_DV:(h + 1) * RET_DV]
                p = (_dot_nt(qh, kh) * dm_ref[h]).astype(BF16)
                qf32 = qh.astype(F32)
                qf = (qf32 * xif_ref[h]).astype(BF16)
                qb = (qf32 * xib_ref[h]).astype(BF16)
                lhs = jnp.concatenate([p, qf, qb], axis=1)
                rhs = jnp.concatenate(
                    [vh, st.astype(BF16), rb_ref[first + c, h * RET_DK:(h + 1) * RET_DK, :]], axis=0)
                y = _dot(lhs, rhs)
                kz = (kh.astype(F32) * zf_ref[h]).astype(BF16)
                st = gcf_ref[h] * st + _dot_tn(kz, vh)
                ms = jnp.mean(y * y, axis=-1, keepdims=True)
                yn = y * lax.rsqrt(ms + EPS)
                sg = sg_ref[0, rs, h * RET_DV:(h + 1) * RET_DV].astype(F32)
                o_ref[0, rs, h * RET_DV:(h + 1) * RET_DV] = (sg * yn).astype(BF16)
            st_ref[h] = st


def _retention(dec, q, k, v, sg, *, cc, nch=4):
    b, s, _ = q.shape
    nch = min(nch, s // cc)
    ns = s // (cc * nch)

    def both(width):
        return pl.BlockSpec((1, cc * nch, width),
                            lambda i, t: (i, jnp.where(t < ns, ns - 1 - t, t - ns), 0))

    def fwd(width):
        return pl.BlockSpec((1, cc * nch, width), lambda i, t: (i, jnp.maximum(t - ns, 0), 0))

    vec = pltpu.VMEM((RET_HEADS, cc, RET_DK), F32)
    gain = pltpu.VMEM((RET_HEADS, 1, RET_DV), F32)
    return pl.pallas_call(
        functools.partial(_retention_kernel, cc=cc, nch=nch, ns=ns),
        grid=(b, 2 * ns),
        in_specs=[_resident(dec.shape), fwd(RET_QK_WIDTH), both(RET_QK_WIDTH), both(RET_V_WIDTH),
                  fwd(RET_V_WIDTH)],
        out_specs=fwd(RET_V_WIDTH),
        out_shape=jax.ShapeDtypeStruct((b, s, RET_V_WIDTH), BF16),
        scratch_shapes=[
            pltpu.VMEM((RET_HEADS, cc, cc), F32), vec, vec, vec, vec, gain, gain,
            pltpu.VMEM((RET_HEADS, RET_DK, RET_DV), F32),
            pltpu.VMEM((s // cc, RET_QK_WIDTH, RET_DV), BF16),
        ],
        compiler_params=_params(2),
        name="retention",
    )(dec, q, k, v, sg)


def _mix_attn_kernel(x_ref, *rest, scale, rows):
    spectrum = rest[:-11]
    (yr_ref, gt_ref, cs_ref, fw_ref, rw_ref, mw_ref, g_ref, wq_ref, kv_ref, wo_ref, o_ref) = rest[-11:]
    d = x_ref.shape[-1]
    heads = [(h * XA_HEAD_DIM, (h + 1) * XA_HEAD_DIM) for h in range(XA_HEADS)]
    sls = _row_slices(x_ref.shape[1], rows)
    xs = []
    for sl in sls:
        groups = []
        for ls in _lane_blocks(spectrum[0].shape[-1]):
            if len(spectrum) == 1:
                pair = _unpack_pair(spectrum[0][0, sl, ls])
            else:
                pair = (spectrum[0][0, sl, ls], spectrum[1][0, sl, ls])
            groups.append(_dot(jnp.concatenate(pair, axis=1).astype(BF16), cs_ref[...]))
        yf = jnp.concatenate(groups, axis=1) * scale
        ya = _dot(yf.astype(BF16), fw_ref[...])
        yb = _dot(yr_ref[0, sl, :], rw_ref[...])
        ga = gt_ref[0, sl, :d].astype(F32)
        gb = gt_ref[0, sl, d:].astype(F32)
        merged = (ga * ya + gb * yb).astype(BF16)
        xs.append(x_ref[0, sl, :] + _dot(merged, mw_ref[...]))
    qs = [_dot(_rmsnorm(x, g_ref[...]).astype(BF16), wq_ref[...]).astype(BF16) for x in xs]
    scores = [[_dot_nt(q[:, lo:hi], kv_ref[0, :, lo:hi]) * (XA_HEAD_DIM ** -0.5) for lo, hi in heads]
              for q in qs]
    probs = []
    for per_head in scores:
        row = []
        for s in per_head:
            e = jnp.exp(s - jnp.max(s, axis=-1, keepdims=True))
            row.append((e / jnp.sum(e, axis=-1, keepdims=True)).astype(BF16))
        probs.append(row)
    outs = [[_dot(p, kv_ref[0, :, d + lo:d + hi]).astype(BF16) for p, (lo, hi) in zip(row, heads)]
            for row in probs]
    for sl, x, per_head in zip(sls, xs, outs):
        o_ref[0, sl, :] = x + _dot(jnp.concatenate(per_head, axis=-1), wo_ref[...])


def _mix_attn(x3, spectrum, yr, gt, cs, fw, rw, mw, gain, wq, kv, wo, *, kv_at, scale,
              tm=1024, rows=512):
    b, s, d = x3.shape
    m = kv.shape[2]
    layer, seq0 = kv_at

    def tok(width):
        return pl.BlockSpec((1, tm, width), lambda i, j: (i, j, 0))

    return pl.pallas_call(
        functools.partial(_mix_attn_kernel, scale=scale, rows=rows),
        grid=(b, s // tm),
        in_specs=[tok(d)] + [tok(FOURIER_WIDTH)] * len(spectrum) + [
            tok(RET_V_WIDTH), tok(2 * d),
            _resident(cs.shape), _resident(fw.shape), _resident(rw.shape), _resident(mw.shape),
            _resident((1, d)), _resident((d, d)),
            pl.BlockSpec((None, 1, m, 2 * d), lambda i, j: (layer, seq0 + i, 0, 0)),
            _resident((d, d))],
        out_specs=tok(d),
        out_shape=jax.ShapeDtypeStruct((b, s, d), F32),
        compiler_params=_params(2),
        name="mix_attn",
    )(x3, *spectrum, yr, gt, cs, fw, rw, mw, gain, wq, kv, wo)


def _mem_kv_kernel(m_ref, g_ref, w_ref, o_ref):
    mn = _rmsnorm(m_ref[...], g_ref[...]).astype(BF16)
    o_ref[...] = _dot(mn, w_ref[...].astype(BF16)).astype(BF16)


def _mem_kv(mem2, gains, wkv, *, steps=3):
    t, d = mem2.shape
    depth, _, n = wkv.shape
    tm = t // steps
    assert tm * steps == t and tm % (2 * _V7X_SUBLANES) == 0, (t, steps)
    return pl.pallas_call(
        _mem_kv_kernel,
        grid=(depth, steps),
        in_specs=[pl.BlockSpec((tm, d), lambda l, i: (i, 0)),
                  pl.BlockSpec((None, 1, d), lambda l, i: (l, 0, 0)),
                  pl.BlockSpec((None, d, n), lambda l, i: (l, 0, 0))],
        out_specs=pl.BlockSpec((None, tm, n), lambda l, i: (l, i, 0)),
        out_shape=jax.ShapeDtypeStruct((depth, t, n), BF16),
        compiler_params=_params(2),
        name="mem_kv",
    )(mem2, gains, wkv)


def _angle(num, n):
    return (2.0 * math.pi / n) * (num % n).astype(F32)


_DFT_TABLE_SPLIT = 32
_DFT_TABLE_DIRECT_MAX = 1024


def _dft_cos_sin(n):
    i = jnp.arange(n, dtype=jnp.int32)
    if n <= _DFT_TABLE_DIRECT_MAX:
        ang = _angle(i[:, None] * i[None, :], n)
        return jnp.cos(ang), jnp.sin(ang)
    na = n // _DFT_TABLE_SPLIT
    ang_a = _angle(i[:, None] * jnp.arange(na, dtype=jnp.int32)[None, :], na)[:, :, None]
    ang_b = _angle(i[:, None] * jnp.arange(_DFT_TABLE_SPLIT, dtype=jnp.int32)[None, :], n)[:, None, :]
    ca, sa, cb, sb = lax.optimization_barrier(
        (jnp.cos(ang_a), jnp.sin(ang_a), jnp.cos(ang_b), jnp.sin(ang_b)))
    return (ca * cb - sa * sb).reshape(n, n), (sa * cb + ca * sb).reshape(n, n)


def _rotary_tables(s):
    half = RET_DK // 2
    inv = 1.0 / (ROPE_BASE ** (jnp.arange(half, dtype=F32) * 2.0 / RET_DK))
    ang = jnp.arange(s, dtype=F32)[:, None] * inv[None, :]
    cos, sin = jnp.cos(ang), jnp.sin(ang)
    return jnp.concatenate([cos, cos], -1), jnp.concatenate([-sin, sin], -1)


def _seq_tables(s):
    tabs = {"rot": _rotary_tables(s)}
    if s >= _DFT_TWO_STAGE_MIN:
        n1, n2 = _DFT_N1, s // _DFT_N1
        c1, s1 = _dft_cos_sin(n1)
        tabs["f1"] = jnp.concatenate([c1, -s1], 0).astype(BF16)
        c2, s2 = _dft_cos_sin(n2)
        tabs["f2"] = jnp.concatenate(
            [jnp.concatenate([c2, s2], 1), jnp.concatenate([-s2, c2], 1)], 0).astype(BF16)
        ang = _angle(jnp.arange(n2, dtype=jnp.int32)[:, None] * jnp.arange(n1, dtype=jnp.int32)[None, :], s)
        lane = (n2, n1, _V7X_LANES)
        tabs["tw"] = (jnp.broadcast_to(jnp.cos(ang)[:, :, None], lane),
                      jnp.broadcast_to(jnp.sin(ang)[:, :, None], lane))
    else:
        c, sn = _dft_cos_sin(s)
        tabs["f"] = jnp.concatenate([c, -sn], 0).astype(BF16)
    return tabs


def _channel_dft_matrix():
    c, s = _dft_cos_sin(FOURIER_GROUP_DIM)
    return jnp.concatenate([c, s], 0).astype(BF16)


_CAST_BY_FFN1 = ("mix_w_in", "ffn2_w_in", "ffn2_w_out")
_CAST_BY_FFN2 = ("ffn1_w_in", "ffn1_w_out", "fourier_w", "ret_w_out", "mix_w_out", "xa_wq", "xa_wo")


def _trunk(x, kv, seq0, small, wb, bd, stacked=None):
    b, s, d = x.shape
    tabs = _seq_tables(s)
    cosf, sinf = tabs["rot"]
    two_stage = "f" not in tabs
    scale = float((s * FOURIER_GROUP_DIM) ** -0.5)
    x2 = x.reshape(b * s, d)
    for l in range(DEPTH):
        sm, lw = small[l], wb.setdefault(l, {})

        def cast_args(names, layer):
            if stacked is None or layer >= DEPTH:
                return {}
            return {"cast": [stacked[n] for n in names], "cast_layer": layer}

        def keep(names, copies, layer):
            if copies:
                wb.setdefault(layer, {}).update(zip(names, copies))

        x2, copies = _ffn(x2, sm["ffn1_norm"], lw["ffn1_w_in"], lw["ffn1_w_out"], small["final_norm"],
                          final_norm=False, **cast_args(_CAST_BY_FFN1, l))
        keep(_CAST_BY_FFN1, copies, l)
        hf, q, k, v, sg, gt = _mix_in(x2.reshape(b, s, d), sm["mix_norm"], lw["mix_w_in"], cosf, sinf,
                                      hf_dtype=F32 if two_stage else BF16)
        if two_stage:
            spectrum = (_dft_stage2(_dft_stage1(hf, tabs["f1"], *tabs["tw"], n1=_DFT_N1), tabs["f2"]),)
        else:
            spectrum = _dft_direct(hf, tabs["f"])
        yr = _retention(sm["dec"], q, k, v, sg, cc=_RET_CHUNK)
        x3 = _mix_attn(x2.reshape(b, s, d), spectrum, yr, gt, bd, lw["fourier_w"], lw["ret_w_out"],
                       lw["mix_w_out"], sm["xa_norm"], lw["xa_wq"], kv, lw["xa_wo"],
                       kv_at=(l, seq0), scale=scale)
        x2, copies = _ffn(x3.reshape(b * s, d), sm["ffn2_norm"], lw["ffn2_w_in"], lw["ffn2_w_out"],
                          small["final_norm"], final_norm=(l == DEPTH - 1),
                          **cast_args(_CAST_BY_FFN2, l + 1))
        keep(_CAST_BY_FFN2, copies, l + 1)
    return x2.reshape(b, s, d)


def kernel(x_prompt, x_sample, mem_prompt, mem_sample, ffn1_norm, ffn1_w_in, ffn1_w_out, mix_norm, mix_w_in, fourier_w, ret_decay_fwd, ret_decay_bwd, ret_w_out, mix_w_out, xa_norm, mem_norm, xa_wq, xa_wkv, xa_wo, ffn2_norm, ffn2_w_in, ffn2_w_out, final_norm):
    row = lambda a: a.reshape(1, -1)
    small = {"final_norm": row(final_norm)}
    for l in range(DEPTH):
        dec = jnp.stack([ret_decay_fwd[l], ret_decay_bwd[l]])
        small[l] = {
            "ffn1_norm": row(ffn1_norm[l]), "mix_norm": row(mix_norm[l]), "xa_norm": row(xa_norm[l]),
            "ffn2_norm": row(ffn2_norm[l]),
            "dec": jnp.broadcast_to(dec[:, :, None, None], (2, RET_HEADS, 1, RET_DV)),
        }
    stacked = {
        "ffn1_w_in": ffn1_w_in, "ffn1_w_out": ffn1_w_out, "mix_w_in": mix_w_in, "fourier_w": fourier_w,
        "ret_w_out": ret_w_out, "mix_w_out": mix_w_out, "xa_wq": xa_wq, "xa_wo": xa_wo,
        "ffn2_w_in": ffn2_w_in, "ffn2_w_out": ffn2_w_out,
    }
    wb = {0: {name: stacked[name][0].astype(BF16) for name in _CAST_BY_FFN2}}
    bd = _channel_dft_matrix()
    d = x_prompt.shape[-1]
    m = mem_prompt.shape[1]
    mem_all = jnp.concatenate([mem_prompt.reshape(-1, d), mem_sample.reshape(-1, d)])
    kv = _mem_kv(mem_all, mem_norm[:, None, :], xa_wkv).reshape(DEPTH, -1, m, 2 * d)
    y_prompt = _trunk(x_prompt, kv, 0, small, wb, bd, stacked)
    y_sample = _trunk(x_sample, kv, mem_prompt.shape[0], small, wb, bd)
    return (y_prompt, y_sample)
```

```python
import functools
import math

import jax
import jax.numpy as jnp
from jax import lax
from jax.experimental import pallas as pl
from jax.experimental.pallas import tpu as pltpu

F32 = jnp.float32
BF16 = jnp.bfloat16

D_MODEL = 1024
DEPTH = 4
FOURIER_GROUP_DIM = 128
FOURIER_WIDTH = 512
RET_HEADS = 4
RET_DK = 128
RET_DV = 256
RET_QK_WIDTH = RET_HEADS * RET_DK
RET_V_WIDTH = RET_HEADS * RET_DV
XA_HEADS = 4
XA_HEAD_DIM = D_MODEL // XA_HEADS
ROPE_BASE = 10000.0
EPS = 1e-6

_C_F = (0, FOURIER_WIDTH)
_C_Q = (_C_F[1], _C_F[1] + RET_QK_WIDTH)
_C_K = (_C_Q[1], _C_Q[1] + RET_QK_WIDTH)
_C_V = (_C_K[1], _C_K[1] + RET_V_WIDTH)
_C_G = (_C_V[1], _C_V[1] + RET_V_WIDTH)
_C_GATES = (_C_G[1], _C_G[1] + 2 * D_MODEL)

_V7X_VMEM_BYTES = 64 * 1024 * 1024
_VMEM_LIMIT = _V7X_VMEM_BYTES - 8 * 1024 * 1024
_V7X_LANES = 128
_V7X_SUBLANES = 8

_RET_CHUNK = 256
_DFT_TWO_STAGE_MIN = 16384
_DFT_N1 = 128


def _params(n_axes):
    return pltpu.CompilerParams(
        dimension_semantics=("arbitrary",) * n_axes, vmem_limit_bytes=_VMEM_LIMIT)


def _resident(shape):
    nd = len(shape)
    return pl.BlockSpec(shape, lambda *_: (0,) * nd, pipeline_mode=pl.Buffered(1))


def _rmsnorm(x, g):
    ms = jnp.mean(x * x, axis=-1, keepdims=True)
    return x * lax.rsqrt(ms + EPS) * g


def _sigmoid(x):
    return 1.0 / (1.0 + jnp.exp(-x))


def _log_sigmoid(x):
    return -(jnp.maximum(-x, 0.0) + jnp.log1p(jnp.exp(-jnp.abs(x))))


def _dot(a, b):
    return jnp.dot(a, b, preferred_element_type=F32)


def _dot_nt(a, b):
    return lax.dot_general(a, b, (((1,), (1,)), ((), ())), preferred_element_type=F32)


def _dot_tn(a, b):
    return lax.dot_general(a, b, (((0,), (0,)), ((), ())), preferred_element_type=F32)


def _row_slices(total, rows):
    return [slice(r * rows, (r + 1) * rows) for r in range(total // rows)]


def _cast_specs(stacked, layer, nsteps, step_of):
    in_specs, out_specs, out_shapes = [], [], []
    for a in stacked:
        _, nrows, ncols = a.shape
        rc = nrows // nsteps
        assert rc * nsteps == nrows and rc % (2 * _V7X_SUBLANES) == 0, (a.shape, nsteps)
        in_specs.append(pl.BlockSpec((None, rc, ncols), lambda *g: (layer, step_of(*g), 0)))
        out_specs.append(pl.BlockSpec((rc, ncols), lambda *g: (step_of(*g), 0)))
        out_shapes.append(jax.ShapeDtypeStruct((nrows, ncols), BF16))
    return in_specs, out_specs, out_shapes


def _cast_chunks(src_refs, dst_refs):
    for src, dst in zip(src_refs, dst_refs):
        dst[...] = src[...].astype(BF16)


def _ffn_kernel(x_ref, g_ref, wg_ref, wu_ref, wo_ref, fg_ref, *rest, rows, tf, final_norm):
    n_cast = (len(rest) - 1) // 2
    o_ref = rest[n_cast]
    _cast_chunks(rest[:n_cast], rest[n_cast + 1:])
    f = wg_ref.shape[1]
    for sl in _row_slices(x_ref.shape[0], rows):
        x = x_ref[sl, :]
        xn = _rmsnorm(x, g_ref[...]).astype(BF16)
        acc = None
        for c in range(f // tf):
            cs = slice(c * tf, (c + 1) * tf)
            gate = _dot(xn, wg_ref[:, cs])
            up = _dot(xn, wu_ref[:, cs])
            act = (gate * _sigmoid(gate) * up).astype(BF16)
            part = _dot(act, wo_ref[cs, :])
            acc = part if acc is None else acc + part
        y = x + 0.5 * acc
        if final_norm:
            y = _rmsnorm(y, fg_ref[...])
        o_ref[sl, :] = y


def _ffn(x2, gain, w_in, w_out, final_gain, *, final_norm, cast=(), cast_layer=0,
         tm=1024, rows=256, tf=2816):
    t, d = x2.shape
    f = w_out.shape[0]
    nsteps = t // tm
    c_in, c_out, c_shapes = _cast_specs(cast, cast_layer, nsteps, lambda i: i)
    outs = pl.pallas_call(
        functools.partial(_ffn_kernel, rows=rows, tf=tf, final_norm=final_norm),
        grid=(nsteps,),
        in_specs=[
            pl.BlockSpec((tm, d), lambda i: (i, 0)),
            _resident((1, d)),
            pl.BlockSpec((d, f), lambda i: (0, 0), pipeline_mode=pl.Buffered(1)),
            pl.BlockSpec((d, f), lambda i: (0, 1), pipeline_mode=pl.Buffered(1)),
            _resident((f, d)), _resident((1, d)),
        ] + c_in,
        out_specs=[pl.BlockSpec((tm, d), lambda i: (i, 0))] + c_out,
        out_shape=[jax.ShapeDtypeStruct((t, d), F32)] + c_shapes,
        compiler_params=_params(1),
        name="ffn",
    )(x2, gain, w_in, w_in, w_out, final_gain, *cast)
    return outs[0], list(outs[1:])


def _mix_in_kernel(x_ref, g_ref, w_ref, cos_ref, sin_ref,
                   hf_ref, q_ref, k_ref, v_ref, sg_ref, gt_ref, *, rows):
    def rotary(hx, out_ref, sl, scale):
        cosf = cos_ref[sl, :]
        sinf = sin_ref[sl, :]
        for h in range(RET_HEADS):
            blk = hx[:, h * RET_DK:(h + 1) * RET_DK]
            rot = blk * cosf + pltpu.roll(blk, RET_DK // 2, 1) * sinf
            if scale != 1.0:
                rot = rot * scale
            out_ref[0, sl, h * RET_DK:(h + 1) * RET_DK] = rot.astype(out_ref.dtype)

    for sl in _row_slices(x_ref.shape[1], rows):
        xn = _rmsnorm(x_ref[0, sl, :], g_ref[...]).astype(BF16)

        def proj(cols):
            return _dot(xn, w_ref[:, cols[0]:cols[1]])

        hf_ref[0, sl, :] = proj(_C_F).astype(hf_ref.dtype)
        rotary(proj(_C_Q), q_ref, sl, 1.0)
        rotary(proj(_C_K), k_ref, sl, RET_DK ** -0.5)
        v_ref[0, sl, :] = proj(_C_V).astype(v_ref.dtype)
        hg = proj(_C_G)
        sg_ref[0, sl, :] = (hg * _sigmoid(hg)).astype(sg_ref.dtype)
        gt_ref[0, sl, :] = _sigmoid(proj(_C_GATES)).astype(gt_ref.dtype)


def _mix_in(x3, gain, w, cosf, sinf, *, hf_dtype, tm=1024, rows=256):
    b, s, d = x3.shape
    n = w.shape[1]

    def tok(width):
        return pl.BlockSpec((1, tm, width), lambda i, j: (i, j, 0))

    def out(width, dtype=BF16):
        return jax.ShapeDtypeStruct((b, s, width), dtype)

    return pl.pallas_call(
        functools.partial(_mix_in_kernel, rows=rows),
        grid=(b, s // tm),
        in_specs=[
            tok(d), _resident((1, d)), _resident((d, n)),
            pl.BlockSpec((tm, RET_DK), lambda i, j: (j, 0)),
            pl.BlockSpec((tm, RET_DK), lambda i, j: (j, 0)),
        ],
        out_specs=[tok(FOURIER_WIDTH), tok(RET_QK_WIDTH), tok(RET_QK_WIDTH),
                   tok(RET_V_WIDTH), tok(RET_V_WIDTH), tok(2 * D_MODEL)],
        out_shape=[out(FOURIER_WIDTH, hf_dtype), out(RET_QK_WIDTH), out(RET_QK_WIDTH),
                   out(RET_V_WIDTH), out(RET_V_WIDTH), out(2 * D_MODEL)],
        compiler_params=_params(2),
        name="mix_in",
    )(x3, gain, w, cosf, sinf)


def _dft_direct_kernel(a_ref, f_ref, xr_ref, xi_ref):
    s = a_ref.shape[1]
    res = _dot(f_ref[...], a_ref[0])
    xr_ref[0] = res[:s].astype(BF16)
    xi_ref[0] = res[s:].astype(BF16)


def _dft_direct(hf, fmat, *, bw=512):
    b, s, c = hf.shape
    spec = pl.BlockSpec((1, s, bw), lambda i, j: (i, 0, j))
    return pl.pallas_call(
        _dft_direct_kernel,
        grid=(b, c // bw),
        in_specs=[spec, _resident((2 * s, s))],
        out_specs=[spec, spec],
        out_shape=[jax.ShapeDtypeStruct((b, s, c), BF16)] * 2,
        compiler_params=_params(2),
        name="dft_direct",
    )(hf, fmat)


def _lane_blocks(c):
    return [slice(q * _V7X_LANES, (q + 1) * _V7X_LANES) for q in range(c // _V7X_LANES)]


def _pack_pair(re, im):
    hi = lax.bitcast_convert_type(re.astype(BF16).astype(F32), jnp.uint32)
    lo = lax.bitcast_convert_type(im.astype(BF16).astype(F32), jnp.uint32)
    return hi | (lo >> 16)


def _unpack_pair(w):
    re = lax.bitcast_convert_type(w & jnp.uint32(0xFFFF0000), F32)
    im = lax.bitcast_convert_type(w << 16, F32)
    return re, im


def _dft_stage1_kernel(a_ref, f_ref, tc_ref, ts_ref, t_ref, a2, t2):
    n1 = a_ref.shape[1]
    slab = (n1, _V7X_SUBLANES, _V7X_LANES)
    for q, ls in enumerate(_lane_blocks(a_ref.shape[-1])):
        a2[q] = a_ref[0, :, :, ls].reshape(a2.shape[1:])
        for j in range(_V7X_SUBLANES):
            pick = pl.ds(j, n1, stride=_V7X_SUBLANES)
            res = _dot(f_ref[...], a2[q, pick, :].astype(BF16))
            ur, ui = res[:n1], res[n1:]
            cw, sw = tc_ref[j], ts_ref[j]
            t2[q, pick, :] = _pack_pair(ur * cw + ui * sw, ui * cw - ur * sw)
        t_ref[0, :, :, ls] = t2[q].reshape(slab)


def _dft_stage1(hf, f1, twc, tws, *, n1):
    b, s, c = hf.shape
    n2 = s // n1
    blk = pl.BlockSpec((1, n1, _V7X_SUBLANES, c), lambda i, j: (i, 0, j, 0))
    tw = pl.BlockSpec((_V7X_SUBLANES, n1, _V7X_LANES), lambda i, j: (j, 0, 0))
    rows = (c // _V7X_LANES, n1 * _V7X_SUBLANES, _V7X_LANES)
    return pl.pallas_call(
        _dft_stage1_kernel,
        grid=(b, n2 // _V7X_SUBLANES),
        in_specs=[blk, _resident((2 * n1, n1)), tw, tw],
        out_specs=blk,
        out_shape=jax.ShapeDtypeStruct((b, n1, n2, c), jnp.uint32),
        scratch_shapes=[pltpu.VMEM(rows, F32), pltpu.VMEM(rows, jnp.uint32)],
        compiler_params=_params(2),
        name="dft_stage1",
    )(hf.reshape(b, n1, n2, c), f1, twc, tws)


def _dft_stage2_kernel(t_ref, f_ref, x_ref, x2):
    n2 = t_ref.shape[2]
    slab = (n2, _V7X_SUBLANES, _V7X_LANES)
    for q, ls in enumerate(_lane_blocks(t_ref.shape[-1])):
        for j in range(_V7X_SUBLANES):
            t = jnp.concatenate(_unpack_pair(t_ref[0, j, :, ls]), axis=0).astype(BF16)
            res = _dot(f_ref[...], t)
            pick = pl.ds(j, n2, stride=_V7X_SUBLANES)
            x2[q, pick, :] = _pack_pair(res[:n2], res[n2:])
        x_ref[0, :, :, ls] = x2[q].reshape(slab)


def _dft_stage2(t, f2):
    b, n1, n2, c = t.shape
    inb = pl.BlockSpec((1, _V7X_SUBLANES, n2, c), lambda i, j: (i, j, 0, 0))
    outb = pl.BlockSpec((1, n2, _V7X_SUBLANES, c), lambda i, j: (i, 0, j, 0))
    rows = (c // _V7X_LANES, n2 * _V7X_SUBLANES, _V7X_LANES)
    xp = pl.pallas_call(
        _dft_stage2_kernel,
        grid=(b, n1 // _V7X_SUBLANES),
        in_specs=[inb, _resident((2 * n2, 2 * n2))],
        out_specs=outb,
        out_shape=jax.ShapeDtypeStruct((b, n2, n1, c), jnp.uint32),
        scratch_shapes=[pltpu.VMEM(rows, jnp.uint32)],
        compiler_params=_params(2),
        name="dft_stage2",
    )(t, f2)
    return xp.reshape(b, n2 * n1, c)


def _retention_kernel(dec_ref, q_ref, k_ref, v_ref, sg_ref, o_ref,
                      dm_ref, xif_ref, xib_ref, zf_ref, zb_ref, gcf_ref, gcb_ref, st_ref, rb_ref,
                      *, cc, nch, ns):
    b = pl.program_id(0)
    t = pl.program_id(1)

    @pl.when((b == 0) & (t == 0))
    def _():
        row = lax.broadcasted_iota(jnp.int32, (cc, cc), 0)
        col = lax.broadcasted_iota(jnp.int32, (cc, cc), 1)
        diff = (row - col).astype(F32)
        rowi = lax.broadcasted_iota(jnp.int32, (cc, RET_DV), 0).astype(F32)
        for h in range(RET_HEADS):
            lgf = _log_sigmoid(dec_ref[0, h])
            lgb = _log_sigmoid(dec_ref[1, h])
            dm_ref[h] = jnp.where(diff >= 0.0,
                                  jnp.exp(lgf * jnp.maximum(diff, 0.0)),
                                  jnp.exp(lgb * jnp.maximum(-diff, 0.0)))
            xif_ref[h] = jnp.exp(lgf * (rowi + 1.0))[:, :RET_DK]
            xib_ref[h] = jnp.exp(lgb * (float(cc) - rowi))[:, :RET_DK]
            zf_ref[h] = jnp.exp(lgf * (float(cc) - 1.0 - rowi))[:, :RET_DK]
            zb_ref[h] = jnp.exp(lgb * rowi)[:, :RET_DK]
            gcf_ref[h] = jnp.exp(lgf * float(cc))
            gcb_ref[h] = jnp.exp(lgb * float(cc))

    @pl.when((t == 0) | (t == ns))
    def _():
        st_ref[...] = jnp.zeros_like(st_ref)

    @pl.when(t < ns)
    def _():
        first = (ns - 1 - t) * nch
        for h in range(RET_HEADS):
            st = st_ref[h]
            for c in reversed(range(nch)):
                rs = slice(c * cc, (c + 1) * cc)
                rb_ref[first + c, h * RET_DK:(h + 1) * RET_DK, :] = st.astype(BF16)
                kh = k_ref[0, rs, h * RET_DK:(h + 1) * RET_DK]
                vh = v_ref[0, rs, h * RET_DV:(h + 1) * RET_DV]
                kz = (kh.astype(F32) * zb_ref[h]).astype(BF16)
                st = gcb_ref[h] * st + _dot_tn(kz, vh)
            st_ref[h] = st

    @pl.when(t >= ns)
    def _():
        first = (t - ns) * nch
        for h in range(RET_HEADS):
            st = st_ref[h]
            for c in range(nch):
                rs = slice(c * cc, (c + 1) * cc)
                qh = q_ref[0, rs, h * RET_DK:(h + 1) * RET_DK]
                kh = k_ref[0, rs, h * RET_DK:(h + 1) * RET_DK]
                vh = v_ref[0, rs, h * RET_DV:(h + 1) * RET_DV]
                p = (_dot_nt(qh, kh) * dm_ref[h]).astype(BF16)
                qf32 = qh.astype(F32)
                qf = (qf32 * xif_ref[h]).astype(BF16)
                qb = (qf32 * xib_ref[h]).astype(BF16)
                lhs = jnp.concatenate([p, qf, qb], axis=1)
                rhs = jnp.concatenate(
                    [vh, st.astype(BF16), rb_ref[first + c, h * RET_DK:(h + 1) * RET_DK, :]], axis=0)
                y = _dot(lhs, rhs)
                kz = (kh.astype(F32) * zf_ref[h]).astype(BF16)
                st = gcf_ref[h] * st + _dot_tn(kz, vh)
                ms = jnp.mean(y * y, axis=-1, keepdims=True)
                yn = y * lax.rsqrt(ms + EPS)
                sg = sg_ref[0, rs, h * RET_DV:(h + 1) * RET_DV].astype(F32)
                o_ref[0, rs, h * RET_DV:(h + 1) * RET_DV] = (sg * yn).astype(BF16)
            st_ref[h] = st


def _retention(dec, q, k, v, sg, *, cc, nch=4):
    b, s, _ = q.shape
    nch = min(nch, s // cc)
    ns = s // (cc * nch)

    def both(width):
        return pl.BlockSpec((1, cc * nch, width),
                            lambda i, t: (i, jnp.where(t < ns, ns - 1 - t, t - ns), 0))

    def fwd(width):
        return pl.BlockSpec((1, cc * nch, width), lambda i, t: (i, jnp.maximum(t - ns, 0), 0))

    vec = pltpu.VMEM((RET_HEADS, cc, RET_DK), F32)
    gain = pltpu.VMEM((RET_HEADS, 1, RET_DV), F32)
    return pl.pallas_call(
        functools.partial(_retention_kernel, cc=cc, nch=nch, ns=ns),
        grid=(b, 2 * ns),
        in_specs=[_resident(dec.shape), fwd(RET_QK_WIDTH), both(RET_QK_WIDTH), both(RET_V_WIDTH),
                  fwd(RET_V_WIDTH)],
        out_specs=fwd(RET_V_WIDTH),
        out_shape=jax.ShapeDtypeStruct((b, s, RET_V_WIDTH), BF16),
        scratch_shapes=[
            pltpu.VMEM((RET_HEADS, cc, cc), F32), vec, vec, vec, vec, gain, gain,
            pltpu.VMEM((RET_HEADS, RET_DK, RET_DV), F32),
            pltpu.VMEM((s // cc, RET_QK_WIDTH, RET_DV), BF16),
        ],
        compiler_params=_params(2),
        name="retention",
    )(dec, q, k, v, sg)


def _mix_attn_kernel(x_ref, *rest, scale, rows):
    spectrum = rest[:-11]
    (yr_ref, gt_ref, cs_ref, fw_ref, rw_ref, mw_ref, g_ref, wq_ref, kv_ref, wo_ref, o_ref) = rest[-11:]
    d = x_ref.shape[-1]
    heads = [(h * XA_HEAD_DIM, (h + 1) * XA_HEAD_DIM) for h in range(XA_HEADS)]
    sls = _row_slices(x_ref.shape[1], rows)
    xs = []
    for sl in sls:
        groups = []
        for ls in _lane_blocks(spectrum[0].shape[-1]):
            if len(spectrum) == 1:
                pair = _unpack_pair(spectrum[0][0, sl, ls])
            else:
                pair = (spectrum[0][0, sl, ls], spectrum[1][0, sl, ls])
            groups.append(_dot(jnp.concatenate(pair, axis=1).astype(BF16), cs_ref[...]))
        yf = jnp.concatenate(groups, axis=1) * scale
        ya = _dot(yf.astype(BF16), fw_ref[...])
        yb = _dot(yr_ref[0, sl, :], rw_ref[...])
        ga = gt_ref[0, sl, :d].astype(F32)
        gb = gt_ref[0, sl, d:].astype(F32)
        merged = (ga * ya + gb * yb).astype(BF16)
        xs.append(x_ref[0, sl, :] + _dot(merged, mw_ref[...]))
    qs = [_dot(_rmsnorm(x, g_ref[...]).astype(BF16), wq_ref[...]).astype(BF16) for x in xs]
    scores = [[_dot_nt(q[:, lo:hi], kv_ref[0, :, lo:hi]) * (XA_HEAD_DIM ** -0.5) for lo, hi in heads]
              for q in qs]
    probs = []
    for per_head in scores:
        row = []
        for s in per_head:
            e = jnp.exp(s - jnp.max(s, axis=-1, keepdims=True))
            row.append((e / jnp.sum(e, axis=-1, keepdims=True)).astype(BF16))
        probs.append(row)
    outs = [[_dot(p, kv_ref[0, :, d + lo:d + hi]).astype(BF16) for p, (lo, hi) in zip(row, heads)]
            for row in probs]
    for sl, x, per_head in zip(sls, xs, outs):
        o_ref[0, sl, :] = x + _dot(jnp.concatenate(per_head, axis=-1), wo_ref[...])


def _mix_attn(x3, spectrum, yr, gt, cs, fw, rw, mw, gain, wq, kv, wo, *, kv_at, scale,
              tm=1024, rows=512):
    b, s, d = x3.shape
    m = kv.shape[2]
    layer, seq0 = kv_at

    def tok(width):
        return pl.BlockSpec((1, tm, width), lambda i, j: (i, j, 0))

    return pl.pallas_call(
        functools.partial(_mix_attn_kernel, scale=scale, rows=rows),
        grid=(b, s // tm),
        in_specs=[tok(d)] + [tok(FOURIER_WIDTH)] * len(spectrum) + [
            tok(RET_V_WIDTH), tok(2 * d),
            _resident(cs.shape), _resident(fw.shape), _resident(rw.shape), _resident(mw.shape),
            _resident((1, d)), _resident((d, d)),
            pl.BlockSpec((None, 1, m, 2 * d), lambda i, j: (layer, seq0 + i, 0, 0)),
            _resident((d, d))],
        out_specs=tok(d),
        out_shape=jax.ShapeDtypeStruct((b, s, d), F32),
        compiler_params=_params(2),
        name="mix_attn",
    )(x3, *spectrum, yr, gt, cs, fw, rw, mw, gain, wq, kv, wo)


def _mem_kv_kernel(m_ref, g_ref, w_ref, o_ref):
    mn = _rmsnorm(m_ref[...], g_ref[...]).astype(BF16)
    o_ref[...] = _dot(mn, w_ref[...].astype(BF16)).astype(BF16)


def _mem_kv(mem2, gains, wkv, *, steps=3):
    t, d = mem2.shape
    depth, _, n = wkv.shape
    tm = t // steps
    assert tm * steps == t and tm % (2 * _V7X_SUBLANES) == 0, (t, steps)
    return pl.pallas_call(
        _mem_kv_kernel,
        grid=(depth, steps),
        in_specs=[pl.BlockSpec((tm, d), lambda l, i: (i, 0)),
                  pl.BlockSpec((None, 1, d), lambda l, i: (l, 0, 0)),
                  pl.BlockSpec((None, d, n), lambda l, i: (l, 0, 0))],
        out_specs=pl.BlockSpec((None, tm, n), lambda l, i: (l, i, 0)),
        out_shape=jax.ShapeDtypeStruct((depth, t, n), BF16),
        compiler_params=_params(2),
        name="mem_kv",
    )(mem2, gains, wkv)


def _angle(num, n):
    return (2.0 * math.pi / n) * (num % n).astype(F32)


_DFT_TABLE_SPLIT = 32
_DFT_TABLE_DIRECT_MAX = 1024


def _dft_cos_sin(n):
    i = jnp.arange(n, dtype=jnp.int32)
    if n <= _DFT_TABLE_DIRECT_MAX:
        ang = _angle(i[:, None] * i[None, :], n)
        return jnp.cos(ang), jnp.sin(ang)
    na = n // _DFT_TABLE_SPLIT
    ang_a = _angle(i[:, None] * jnp.arange(na, dtype=jnp.int32)[None, :], na)[:, :, None]
    ang_b = _angle(i[:, None] * jnp.arange(_DFT_TABLE_SPLIT, dtype=jnp.int32)[None, :], n)[:, None, :]
    ca, sa, cb, sb = lax.optimization_barrier(
        (jnp.cos(ang_a), jnp.sin(ang_a), jnp.cos(ang_b), jnp.sin(ang_b)))
    return (ca * cb - sa * sb).reshape(n, n), (sa * cb + ca * sb).reshape(n, n)


def _rotary_tables(s):
    half = RET_DK // 2
    inv = 1.0 / (ROPE_BASE ** (jnp.arange(half, dtype=F32) * 2.0 / RET_DK))
    ang = jnp.arange(s, dtype=F32)[:, None] * inv[None, :]
    cos, sin = jnp.cos(ang), jnp.sin(ang)
    return jnp.concatenate([cos, cos], -1), jnp.concatenate([-sin, sin], -1)


def _seq_tables(s):
    tabs = {"rot": _rotary_tables(s)}
    if s >= _DFT_TWO_STAGE_MIN:
        n1, n2 = _DFT_N1, s // _DFT_N1
        c1, s1 = _dft_cos_sin(n1)
        tabs["f1"] = jnp.concatenate([c1, -s1], 0).astype(BF16)
        c2, s2 = _dft_cos_sin(n2)
        tabs["f2"] = jnp.concatenate(
            [jnp.concatenate([c2, s2], 1), jnp.concatenate([-s2, c2], 1)], 0).astype(BF16)
        ang = _angle(jnp.arange(n2, dtype=jnp.int32)[:, None] * jnp.arange(n1, dtype=jnp.int32)[None, :], s)
        lane = (n2, n1, _V7X_LANES)
        tabs["tw"] = (jnp.broadcast_to(jnp.cos(ang)[:, :, None], lane),
                      jnp.broadcast_to(jnp.sin(ang)[:, :, None], lane))
    else:
        c, sn = _dft_cos_sin(s)
        tabs["f"] = jnp.concatenate([c, -sn], 0).astype(BF16)
    return tabs


def _channel_dft_matrix():
    c, s = _dft_cos_sin(FOURIER_GROUP_DIM)
    return jnp.concatenate([c, s], 0).astype(BF16)


_CAST_BY_FFN1 = ("mix_w_in", "ffn2_w_in", "ffn2_w_out")
_CAST_BY_FFN2 = ("ffn1_w_in", "ffn1_w_out", "fourier_w", "ret_w_out", "mix_w_out", "xa_wq", "xa_wo")


def _trunk(x, kv, seq0, small, wb, bd, stacked=None):
    b, s, d = x.shape
    tabs = _seq_tables(s)
    cosf, sinf = tabs["rot"]
    two_stage = "f" not in tabs
    scale = float((s * FOURIER_GROUP_DIM) ** -0.5)
    x2 = x.reshape(b * s, d)
    for l in range(DEPTH):
        sm, lw = small[l], wb.setdefault(l, {})

        def cast_args(names, layer):
            if stacked is None or layer >= DEPTH:
                return {}
            return {"cast": [stacked[n] for n in names], "cast_layer": layer}

        def keep(names, copies, layer):
            if copies:
                wb.setdefault(layer, {}).update(zip(names, copies))

        x2, copies = _ffn(x2, sm["ffn1_norm"], lw["ffn1_w_in"], lw["ffn1_w_out"], small["final_norm"],
                          final_norm=False, **cast_args(_CAST_BY_FFN1, l))
        keep(_CAST_BY_FFN1, copies, l)
        hf, q, k, v, sg, gt = _mix_in(x2.reshape(b, s, d), sm["mix_norm"], lw["mix_w_in"], cosf, sinf,
                                      hf_dtype=F32 if two_stage else BF16)
        if two_stage:
            spectrum = (_dft_stage2(_dft_stage1(hf, tabs["f1"], *tabs["tw"], n1=_DFT_N1), tabs["f2"]),)
        else:
            spectrum = _dft_direct(hf, tabs["f"])
        yr = _retention(sm["dec"], q, k, v, sg, cc=_RET_CHUNK)
        x3 = _mix_attn(x2.reshape(b, s, d), spectrum, yr, gt, bd, lw["fourier_w"], lw["ret_w_out"],
                       lw["mix_w_out"], sm["xa_norm"], lw["xa_wq"], kv, lw["xa_wo"],
                       kv_at=(l, seq0), scale=scale)
        x2, copies = _ffn(x3.reshape(b * s, d), sm["ffn2_norm"], lw["ffn2_w_in"], lw["ffn2_w_out"],
                          small["final_norm"], final_norm=(l == DEPTH - 1),
                          **cast_args(_CAST_BY_FFN2, l + 1))
        keep(_CAST_BY_FFN2, copies, l + 1)
    return x2.reshape(b, s, d)


def kernel(x_prompt, x_sample, mem_prompt, mem_sample, ffn1_norm, ffn1_w_in, ffn1_w_out, mix_norm, mix_w_in, fourier_w, ret_decay_fwd, ret_decay_bwd, ret_w_out, mix_w_out, xa_norm, mem_norm, xa_wq, xa_wkv, xa_wo, ffn2_norm, ffn2_w_in, ffn2_w_out, final_norm):
    row = lambda a: a.reshape(1, -1)
    small = {"final_norm": row(final_norm)}
    for l in range(DEPTH):
        dec = jnp.stack([ret_decay_fwd[l], ret_decay_bwd[l]])
        small[l] = {
            "ffn1_norm": row(ffn1_norm[l]), "mix_norm": row(mix_norm[l]), "xa_norm": row(xa_norm[l]),
            "ffn2_norm": row(ffn2_norm[l]),
            "dec": jnp.broadcast_to(dec[:, :, None, None], (2, RET_HEADS, 1, RET_DV)),
        }
    stacked = {
        "ffn1_w_in": ffn1_w_in, "ffn1_w_out": ffn1_w_out, "mix_w_in": mix_w_in, "fourier_w": fourier_w,
        "ret_w_out": ret_w_out, "mix_w_out": mix_w_out, "xa_wq": xa_wq, "xa_wo": xa_wo,
        "ffn2_w_in": ffn2_w_in, "ffn2_w_out": ffn2_w_out,
    }
    wb = {0: {name: stacked[name][0].astype(BF16) for name in _CAST_BY_FFN2}}
    bd = _channel_dft_matrix()
    d = x_prompt.shape[-1]
    m = mem_prompt.shape[1]
    mem_all = jnp.concatenate([mem_prompt.reshape(-1, d), mem_sample.reshape(-1, d)])
    kv = _mem_kv(mem_all, mem_norm[:, None, :], xa_wkv).reshape(DEPTH, -1, m, 2 * d)
    y_prompt = _trunk(x_prompt, kv, 0, small, wb, bd, stacked)
    y_sample = _trunk(x_sample, kv, mem_prompt.shape[0], small, wb, bd)
    return (y_prompt, y_sample)
```
